```python
import math
import jax, jax.numpy as jnp
from jax import lax
import numpy as np

D_MODEL = 2048
BATCH = 4
SEQ = 2048
DEPTH = 1
DEC_BATCH = 32
DEC_SEQ = 4
PAST_LEN = 8192
PAGE_SIZE = 128

N_HEADS = 8
HEAD_DIM = 128
V_DIM = 2 * HEAD_DIM
ATTN_QK_WIDTH = N_HEADS * 2 * HEAD_DIM
ATTN_V_WIDTH = N_HEADS * V_DIM
ROT_DIM = HEAD_DIM // 4
ROPE_THETA = 500000.0
ATTN_SCALE = HEAD_DIM ** -0.5
Q_BLOCK = 128
NEG_INF = -1e30
D_CONV = D_MODEL // 2
CONV_WIDTH = 31
N_GROUPS = 4
EXPERTS_PER_GROUP = 8
N_EXPERTS = N_GROUPS * EXPERTS_PER_GROUP
TOP_K = 2
D_FF_EXPERT = D_MODEL // 4
N_IN = 2 * D_CONV + 2 * ATTN_QK_WIDTH + ATTN_V_WIDTH + 2 * D_MODEL
SPLITS = [D_CONV, 2 * D_CONV, 2 * D_CONV + ATTN_QK_WIDTH, 2 * D_CONV + 2 * ATTN_QK_WIDTH,
          2 * D_CONV + 2 * ATTN_QK_WIDTH + ATTN_V_WIDTH,
          2 * D_CONV + 2 * ATTN_QK_WIDTH + ATTN_V_WIDTH + D_MODEL]
EPS = 1e-6

kernel_name = "hybrid_conformer_diffattn_hmoe_step"


def rmsnorm(x, g):
    xf = x.astype(jnp.float32)
    y = xf * lax.rsqrt(jnp.mean(xf * xf, axis=-1, keepdims=True) + EPS)
    return (y * g.astype(jnp.float32)).astype(x.dtype)


def layernorm(x, g, b):
    xf = x.astype(jnp.float32)
    mu = jnp.mean(xf, axis=-1, keepdims=True)
    xc = xf - mu
    var = jnp.mean(xc * xc, axis=-1, keepdims=True)
    y = xc * lax.rsqrt(var + EPS) * g.astype(jnp.float32) + b.astype(jnp.float32)
    return y.astype(x.dtype)


def rotary(t, pos):
    half = ROT_DIM // 2
    inv_freq = ROPE_THETA ** (-jnp.arange(0, ROT_DIM, 2, dtype=jnp.float32) / ROT_DIM)
    ang = pos[:, None] * inv_freq[None, :]
    cos = jnp.cos(ang)[None, :, None, None, :].astype(t.dtype)
    sin = jnp.sin(ang)[None, :, None, None, :].astype(t.dtype)
    t1 = t[..., :half]
    t2 = t[..., half:ROT_DIM]
    return jnp.concatenate([t1 * cos - t2 * sin, t2 * cos + t1 * sin, t[..., ROT_DIM:]], axis=-1)


def in_projection(x, pos, norm1_g, w_in, q_norm_g, k_norm_g):
    B, T, _ = x.shape
    h = rmsnorm(x, norm1_g)
    proj = jnp.einsum('btd,dn->btn', h, w_in)
    u_a, u_b, q, k, v, g_conv, g_attn = jnp.split(proj, SPLITS, axis=-1)
    u = u_a * jax.nn.sigmoid(u_b)
    q = rotary(rmsnorm(q.reshape(B, T, N_HEADS, 2, HEAD_DIM), q_norm_g), pos)
    k = rotary(rmsnorm(k.reshape(B, T, N_HEADS, 2, HEAD_DIM), k_norm_g), pos)
    v = v.reshape(B, T, N_HEADS, V_DIM)
    return u, q, k, v, g_conv, g_attn


def conv_branch(u_hist, conv_w, conv_b, conv_ln_g, conv_ln_b, w_conv_out):
    y = lax.conv_general_dilated(u_hist, conv_w[:, None, :].astype(u_hist.dtype),
                                 window_strides=(1,), padding='VALID',
                                 dimension_numbers=('NWC', 'WIO', 'NWC'),
                                 feature_group_count=D_CONV)
    y = jax.nn.silu(layernorm(y + conv_b, conv_ln_g, conv_ln_b))
    return jnp.einsum('btc,cd->btd', y, w_conv_out)


def diff_lambda(lambda_q1, lambda_k1, lambda_q2, lambda_k2, lambda_init):
    f = lambda a, b: jnp.exp(jnp.sum(a.astype(jnp.float32) * b.astype(jnp.float32)))
    return f(lambda_q1, lambda_k1) - f(lambda_q2, lambda_k2) + lambda_init


def diff_attn_core(q, k, v, mask, lam):
    s = jnp.einsum('bqhcd,bkhcd->bhcqk', q, k, preferred_element_type=jnp.float32) * ATTN_SCALE
    s = jnp.where(mask[None, None, None], s, NEG_INF)
    p = jax.nn.softmax(s, axis=-1)
    a = p[:, :, 0] - lam * p[:, :, 1]
    return jnp.einsum('bhqk,bkhv->bqhv', a.astype(v.dtype), v)


def prompt_attention(q, k, v, lam):
    B, T = q.shape[:2]
    nb = T // Q_BLOCK
    qb = q.reshape(B, nb, Q_BLOCK, N_HEADS, 2, HEAD_DIM).transpose(1, 0, 2, 3, 4, 5)
    qpos = jnp.arange(T).reshape(nb, Q_BLOCK)
    kpos = jnp.arange(T)

    def block(args):
        qblk, qp = args
        return diff_attn_core(qblk, k, v, kpos[None, :] <= qp[:, None], lam)

    o = lax.map(block, (qb, qpos))
    return o.transpose(1, 0, 2, 3, 4).reshape(B, T, N_HEADS, V_DIM)


def sample_attention(q, k_new, v_new, cache_k, cache_v, page_table, lam):
    Ts = q.shape[1]
    kpos = jnp.arange(PAST_LEN + Ts)
    qpos = PAST_LEN + jnp.arange(Ts)
    mask = kpos[None, :] <= qpos[:, None]

    def one(args):
        qs, ks, vs, pt = args
        k_all = jnp.concatenate([cache_k[pt].reshape(PAST_LEN, N_HEADS, 2, HEAD_DIM).astype(ks.dtype), ks], axis=0)
        v_all = jnp.concatenate([cache_v[pt].reshape(PAST_LEN, N_HEADS, V_DIM).astype(vs.dtype), vs], axis=0)
        return diff_attn_core(qs[None], k_all[None], v_all[None], mask, lam)[0]

    return lax.map(one, (q, k_new, v_new, page_table))


def attn_branch_out(o, subln_g, w_attn_out, lambda_init):
    B, T = o.shape[:2]
    o = rmsnorm(o, subln_g) * (1.0 - lambda_init)
    return jnp.einsum('btn,nd->btd', o.reshape(B, T, ATTN_V_WIDTH), w_attn_out)


def hier_moe(h, w_router_group, b_router_group, w_router_expert, b_router_expert,
             w_gate_e, w_up_e, w_down_e):
    T = h.shape[0]
    lg = jnp.einsum('td,dg->tg', h, w_router_group, preferred_element_type=jnp.float32) + b_router_group.astype(jnp.float32)
    pg = jax.nn.softmax(lg, axis=-1)
    g_idx = jnp.argmax(pg, axis=-1)
    p_top = jnp.take_along_axis(pg, g_idx[:, None], axis=-1)[:, 0]
    le = jnp.einsum('td,de->te', h, w_router_expert, preferred_element_type=jnp.float32) + b_router_expert.astype(jnp.float32)
    le = jnp.take_along_axis(le.reshape(T, N_GROUPS, EXPERTS_PER_GROUP), g_idx[:, None, None], axis=1)[:, 0]
    pe = jax.nn.softmax(le, axis=-1)
    w_top, e_loc = lax.top_k(pe, TOP_K)
    w_top = w_top / jnp.sum(w_top, axis=-1, keepdims=True) * p_top[:, None]
    e_glob = g_idx[:, None] * EXPERTS_PER_GROUP + e_loc
    combine = jnp.sum(jax.nn.one_hot(e_glob, N_EXPERTS, dtype=jnp.float32) * w_top[..., None], axis=1)
    out = jnp.zeros((T, D_MODEL), jnp.float32)
    for g in range(N_GROUPS):
        sl = slice(g * EXPERTS_PER_GROUP, (g + 1) * EXPERTS_PER_GROUP)
        hg = jnp.einsum('td,edf->tef', h, w_gate_e[sl])
        hu = jnp.einsum('td,edf->tef', h, w_up_e[sl])
        act = jax.nn.silu(hg) * hu * combine[:, sl, None].astype(h.dtype)
        out = out + jnp.einsum('tef,efd->td', act, w_down_e[sl], preferred_element_type=jnp.float32)
    return out.astype(h.dtype)


def merge_and_ffn(x, y_conv, y_attn, g_conv, g_attn, w_out, norm2_g, w_router_group, b_router_group,
                  w_router_expert, b_router_expert, w_gate_e, w_up_e, w_down_e):
    m = jax.nn.sigmoid(g_conv) * y_conv + jax.nn.sigmoid(g_attn) * y_attn
    x1 = x + jnp.einsum('btd,de->bte', m, w_out)
    h2 = rmsnorm(x1, norm2_g).reshape(-1, D_MODEL)
    ff = hier_moe(h2, w_router_group, b_router_group, w_router_expert, b_router_expert,
                  w_gate_e, w_up_e, w_down_e)
    return x1 + ff.reshape(x1.shape)


def layer_step(x_p, x_s, cache_k, cache_v, state_conv, page_table, lambda_init,
               norm1_g, w_in, conv_w, conv_b, conv_ln_g, conv_ln_b, w_conv_out,
               q_norm_g, k_norm_g, lambda_q1, lambda_k1, lambda_q2, lambda_k2, subln_g,
               w_attn_out, w_out, norm2_g, w_router_group, b_router_group,
               w_router_expert, b_router_expert, w_gate_e, w_up_e, w_down_e):
    B, T = x_p.shape[:2]
    Ts = x_s.shape[1]
    lam = diff_lambda(lambda_q1, lambda_k1, lambda_q2, lambda_k2, lambda_init)
    pos_p = jnp.arange(T, dtype=jnp.float32)
    pos_s = PAST_LEN + jnp.arange(Ts, dtype=jnp.float32)
    u_p, q_p, k_p, v_p, gc_p, ga_p = in_projection(x_p, pos_p, norm1_g, w_in, q_norm_g, k_norm_g)
    u_s, q_s, k_s, v_s, gc_s, ga_s = in_projection(x_s, pos_s, norm1_g, w_in, q_norm_g, k_norm_g)
    u_hist_p = jnp.concatenate([jnp.zeros((B, CONV_WIDTH - 1, D_CONV), u_p.dtype), u_p], axis=1)
    u_hist_s = jnp.concatenate([state_conv.astype(u_s.dtype), u_s], axis=1)
    yc_p = conv_branch(u_hist_p, conv_w, conv_b, conv_ln_g, conv_ln_b, w_conv_out)
    yc_s = conv_branch(u_hist_s, conv_w, conv_b, conv_ln_g, conv_ln_b, w_conv_out)
    ya_p = attn_branch_out(prompt_attention(q_p, k_p, v_p, lam), subln_g, w_attn_out, lambda_init)
    ya_s = attn_branch_out(sample_attention(q_s, k_s, v_s, cache_k, cache_v, page_table, lam),
                           subln_g, w_attn_out, lambda_init)
    y_p = merge_and_ffn(x_p, yc_p, ya_p, gc_p, ga_p, w_out, norm2_g, w_router_group, b_router_group,
                        w_router_expert, b_router_expert, w_gate_e, w_up_e, w_down_e)
    y_s = merge_and_ffn(x_s, yc_s, ya_s, gc_s, ga_s, w_out, norm2_g, w_router_group, b_router_group,
                        w_router_expert, b_router_expert, w_gate_e, w_up_e, w_down_e)
    return (y_p, y_s, k_p, v_p, u_hist_p[:, -(CONV_WIDTH - 1):], k_s, v_s, u_hist_s[:, -(CONV_WIDTH - 1):])


def setup_inputs(seed: int = 0) -> dict:
    key = jax.random.key(seed)
    ks = jax.random.split(key, 32)
    f32 = jnp.float32
    nrm = lambda k, shape, scale: jax.random.normal(k, shape, f32) * scale
    L = DEPTH
    n_pages = PAST_LEN // PAGE_SIZE
    n_used = DEC_BATCH * n_pages
    n_pool = n_used + n_used // 4
    page_table = jax.random.permutation(ks[5], n_pool)[:n_used].reshape(DEC_BATCH, n_pages).astype(jnp.int32)
    return {
        'x_prompt': nrm(ks[0], (BATCH, SEQ, D_MODEL), 1.0),
        'x_sample': nrm(ks[1], (DEC_BATCH, DEC_SEQ, D_MODEL), 1.0),
        'cache_k': nrm(ks[2], (L, n_pool, PAGE_SIZE, N_HEADS, 2, HEAD_DIM), 1.0),
        'cache_v': nrm(ks[3], (L, n_pool, PAGE_SIZE, N_HEADS, V_DIM), 1.0),
        'state_conv': nrm(ks[4], (L, DEC_BATCH, CONV_WIDTH - 1, D_CONV), 0.5),
        'page_table': page_table,
        'norm1_g': 1.0 + nrm(ks[6], (L, D_MODEL), 0.02),
        'w_in': nrm(ks[7], (L, D_MODEL, N_IN), D_MODEL ** -0.5),
        'conv_w': nrm(ks[8], (L, CONV_WIDTH, D_CONV), CONV_WIDTH ** -0.5),
        'conv_b': nrm(ks[9], (L, D_CONV), 0.02),
        'conv_ln_g': 1.0 + nrm(ks[10], (L, D_CONV), 0.02),
        'conv_ln_b': nrm(ks[11], (L, D_CONV), 0.02),
        'w_conv_out': nrm(ks[12], (L, D_CONV, D_MODEL), D_CONV ** -0.5),
        'q_norm_g': 1.0 + nrm(ks[13], (L, HEAD_DIM), 0.02),
        'k_norm_g': 1.0 + nrm(ks[14], (L, HEAD_DIM), 0.02),
        'lambda_q1': nrm(ks[15], (L, HEAD_DIM), 0.1),
        'lambda_k1': nrm(ks[16], (L, HEAD_DIM), 0.1),
        'lambda_q2': nrm(ks[17], (L, HEAD_DIM), 0.1),
        'lambda_k2': nrm(ks[18], (L, HEAD_DIM), 0.1),
        'subln_g': 1.0 + nrm(ks[19], (L, V_DIM), 0.02),
        'w_attn_out': nrm(ks[20], (L, ATTN_V_WIDTH, D_MODEL), ATTN_V_WIDTH ** -0.5),
        'w_out': nrm(ks[21], (L, D_MODEL, D_MODEL), D_MODEL ** -0.5),
        'norm2_g': 1.0 + nrm(ks[22], (L, D_MODEL), 0.02),
        'w_router_group': nrm(ks[23], (L, D_MODEL, N_GROUPS), D_MODEL ** -0.5),
        'b_router_group': nrm(ks[24], (L, N_GROUPS), 0.01),
        'w_router_expert': nrm(ks[25], (L, D_MODEL, N_EXPERTS), D_MODEL ** -0.5),
        'b_router_expert': nrm(ks[26], (L, N_EXPERTS), 0.01),
        'w_gate_e': nrm(ks[27], (L, N_EXPERTS, D_MODEL, D_FF_EXPERT), D_MODEL ** -0.5),
        'w_up_e': nrm(ks[28], (L, N_EXPERTS, D_MODEL, D_FF_EXPERT), D_MODEL ** -0.5),
        'w_down_e': nrm(ks[29], (L, N_EXPERTS, D_FF_EXPERT, D_MODEL), D_FF_EXPERT ** -0.5),
    }


def reference(x_prompt, x_sample, cache_k, cache_v, state_conv, page_table,
              norm1_g, w_in, conv_w, conv_b, conv_ln_g, conv_ln_b, w_conv_out,
              q_norm_g, k_norm_g, lambda_q1, lambda_k1, lambda_q2, lambda_k2, subln_g,
              w_attn_out, w_out, norm2_g, w_router_group, b_router_group,
              w_router_expert, b_router_expert, w_gate_e, w_up_e, w_down_e):
    x_p, x_s = x_prompt, x_sample
    kps, vps, cps, kss, vss, css = [], [], [], [], [], []
    for l in range(DEPTH):
        lambda_init = 0.8 - 0.6 * math.exp(-0.3 * l)
        x_p, x_s, kp, vp, cp, k_s, v_s, c_s = layer_step(
            x_p, x_s, cache_k[l], cache_v[l], state_conv[l], page_table, lambda_init,
            norm1_g[l], w_in[l], conv_w[l], conv_b[l], conv_ln_g[l], conv_ln_b[l], w_conv_out[l],
            q_norm_g[l], k_norm_g[l], lambda_q1[l], lambda_k1[l], lambda_q2[l], lambda_k2[l], subln_g[l],
            w_attn_out[l], w_out[l], norm2_g[l], w_router_group[l], b_router_group[l],
            w_router_expert[l], b_router_expert[l], w_gate_e[l], w_up_e[l], w_down_e[l])
        kps.append(kp); vps.append(vp); cps.append(cp)
        kss.append(k_s); vss.append(v_s); css.append(c_s)
    return (x_p, x_s, jnp.stack(kps), jnp.stack(vps), jnp.stack(cps),
            jnp.stack(kss), jnp.stack(vss), jnp.stack(css))
```

```python
import functools
import math

import jax
import jax.numpy as jnp
from jax import lax
from jax.experimental import pallas as pl
from jax.experimental.pallas import tpu as pltpu

F32 = jnp.float32
BF16 = jnp.bfloat16

D_MODEL = 2048
N_HEADS = 8
HEAD_DIM = 128
V_DIM = 2 * HEAD_DIM
QK_WIDTH = N_HEADS * 2 * HEAD_DIM
V_WIDTH = N_HEADS * V_DIM
ROT_DIM = HEAD_DIM // 4
ROPE_THETA = 500000.0
ATTN_SCALE = HEAD_DIM ** -0.5
NEG_INF = -1e30
D_CONV = D_MODEL // 2
CONV_WIDTH = 31
HIST = CONV_WIDTH - 1
N_GROUPS = 4
EXPERTS_PER_GROUP = 8
N_EXPERTS = N_GROUPS * EXPERTS_PER_GROUP
TOP_K = 2
D_FF = D_MODEL // 4
EPS = 1e-6
PAGE = 128

LANES = 128
SUBLANES = 8
ROW_TILES = D_MODEL // LANES
CONV_TILES = D_CONV // LANES
VMEM_LIMIT = 56 * 1024 * 1024

TM = 1664
TN = 512
TQ = 512
PAGES_PER_STEP = 4
MOE_TM = 256
CMB_TM = 320
CONV_ROWS = 16
NEW_ROWS = 16


def _cparams(sem):
    return pltpu.CompilerParams(dimension_semantics=sem, vmem_limit_bytes=VMEM_LIMIT)


def _rows_to_tile(buf, base, rows, tiles):
    return jnp.concatenate([buf[pl.ds(base + c, rows, stride=tiles), :] for c in range(tiles)], axis=1)


def _rmsnorm_kernel(x_ref, g_ref, o_ref):
    x = x_ref[...]
    ms = jnp.mean(x * x, axis=-1, keepdims=True)
    o_ref[...] = ((x * lax.rsqrt(ms + EPS)) * g_ref[...]).astype(o_ref.dtype)


def _rmsnorm(x, g, tm):
    t, d = x.shape
    return pl.pallas_call(
        _rmsnorm_kernel,
        grid=(t // tm,),
        in_specs=[pl.BlockSpec((tm, d), lambda i: (i, 0)), pl.BlockSpec((1, d), lambda i: (0, 0))],
        out_specs=pl.BlockSpec((tm, d), lambda i: (i, 0)),
        out_shape=jax.ShapeDtypeStruct((t, d), BF16),
        compiler_params=_cparams(("arbitrary",)),
        name="rmsnorm1",
    )(x, g.reshape(1, d))


def _cache_weight(w_ref, w_s):
    @pl.when(pl.program_id(1) == 0)
    def _():
        w_s[...] = w_ref[...].astype(BF16)


def _glu_kernel(h_ref, wa_ref, wb_ref, u_ref):
    h = h_ref[...]
    a = jnp.dot(h, wa_ref[...].astype(BF16), preferred_element_type=F32)
    b = jnp.dot(h, wb_ref[...].astype(BF16), preferred_element_type=F32)
    u = a * jax.nn.sigmoid(b)
    j = pl.program_id(1)
    tiles = TN // LANES
    for c in range(tiles):
        u_ref[pl.ds(j * tiles + c, TM, stride=CONV_TILES), :] = u[:, c * LANES:(c + 1) * LANES]


def _glu_call(h, w, name):
    t, k = h.shape
    nj = D_CONV // TN
    return pl.pallas_call(
        _glu_kernel,
        grid=(t // TM, nj),
        in_specs=[pl.BlockSpec((TM, k), lambda i, j: (i, 0)), pl.BlockSpec((k, TN), lambda i, j: (0, j)),
                  pl.BlockSpec((k, TN), lambda i, j: (0, nj + j))],
        out_specs=pl.BlockSpec((TM * CONV_TILES, LANES), lambda i, j: (i, 0)),
        out_shape=jax.ShapeDtypeStruct((t * CONV_TILES, LANES), F32),
        compiler_params=_cparams(("arbitrary", "arbitrary")),
        name=name,
    )(h, w, w)


def _head_norm_rotary(acc, gain, tcos, tsin, lane):
    outs = []
    for g in range(acc.shape[1] // LANES):
        t = acc[:, g * LANES:(g + 1) * LANES]
        ms = jnp.mean(t * t, axis=-1, keepdims=True)
        t = (t * lax.rsqrt(ms + EPS)) * gain
        half = ROT_DIM // 2
        partner = jnp.where(lane < half, pltpu.roll(t, LANES - half, 1), pltpu.roll(t, half, 1))
        outs.append(t * tcos + partner * tsin)
    return outs


def _q_kernel(h_ref, w_ref, g_ref, tc_ref, ts_ref, ob_ref, w_s):
    _cache_weight(w_ref, w_s)
    acc = jnp.dot(h_ref[...], w_s[...], preferred_element_type=F32)
    lane = lax.broadcasted_iota(jnp.int32, (acc.shape[0], LANES), 1)
    outs = _head_norm_rotary(acc, g_ref[...], tc_ref[...], ts_ref[...], lane)
    for g, y in enumerate(outs):
        ob_ref[:, g * LANES:(g + 1) * LANES] = (y * ATTN_SCALE).astype(BF16)


def _k_kernel(h_ref, w_ref, g_ref, tc_ref, ts_ref, of_ref, ob_ref, w_s):
    _cache_weight(w_ref, w_s)
    acc = jnp.dot(h_ref[...], w_s[...], preferred_element_type=F32)
    lane = lax.broadcasted_iota(jnp.int32, (acc.shape[0], LANES), 1)
    outs = _head_norm_rotary(acc, g_ref[...], tc_ref[...], ts_ref[...], lane)
    for g, y in enumerate(outs):
        of_ref[:, g * LANES:(g + 1) * LANES] = y
        ob_ref[:, g * LANES:(g + 1) * LANES] = y.astype(BF16)


def _v_kernel(h_ref, w_ref, of_ref, ob_ref, w_s):
    _cache_weight(w_ref, w_s)
    acc = jnp.dot(h_ref[...], w_s[...], preferred_element_type=F32)
    of_ref[...] = acc
    ob_ref[...] = acc.astype(BF16)


def _gate_kernel(h_ref, w_ref, o_ref, w_s):
    _cache_weight(w_ref, w_s)
    acc = jnp.dot(h_ref[...], w_s[...], preferred_element_type=F32)
    o_ref[...] = jax.nn.sigmoid(acc)


def _proj_call(kernel, h, w, col0, width, extra_in, extra_specs, out_dtypes, name):
    t, k = h.shape
    nj, ni = width // TN, t // TM
    out_specs = [pl.BlockSpec((TM, TN), lambda j, i: (i, j)) for _ in out_dtypes]
    out_shape = [jax.ShapeDtypeStruct((t, width), dt) for dt in out_dtypes]
    return pl.pallas_call(
        kernel,
        grid=(nj, ni),
        in_specs=[pl.BlockSpec((TM, k), lambda j, i: (i, 0)),
                  pl.BlockSpec((k, TN), lambda j, i: (0, col0 // TN + j))] + extra_specs,
        out_specs=out_specs,
        out_shape=out_shape,
        scratch_shapes=[pltpu.VMEM((k, TN), BF16)],
        compiler_params=_cparams(("arbitrary", "arbitrary")),
        name=name,
    )(h, w, *extra_in)


def _conv_kernel(hist_ref, w_ref, b_ref, g_ref, beta_ref, o_ref, *, length):
    rows = min(CONV_ROWS, length)

    def step(i, carry):
        t0 = i * rows
        acc = jnp.zeros((rows, CONV_TILES, LANES), F32)
        for j in range(CONV_WIDTH):
            acc = acc + hist_ref[0, pl.ds(t0 + j, rows)] * w_ref[j]
        y = acc + b_ref[...]
        mu = jnp.mean(y, axis=(1, 2), keepdims=True)
        yc = y - mu
        var = jnp.mean(yc * yc, axis=(1, 2), keepdims=True)
        z = yc * lax.rsqrt(var + EPS) * g_ref[...] + beta_ref[...]
        o_ref[0, pl.ds(t0, rows)] = z * jax.nn.sigmoid(z)
        return carry

    lax.fori_loop(0, length // rows, step, 0)


def _conv_branch(hist, conv_w, conv_b, ln_g, ln_b, name):
    b, lh = hist.shape[:2]
    length = lh - HIST
    tile = lambda a: a.reshape(-1, CONV_TILES, LANES)
    whole = lambda n: pl.BlockSpec((n, CONV_TILES, LANES), lambda i: (0, 0, 0))
    return pl.pallas_call(
        functools.partial(_conv_kernel, length=length),
        grid=(b,),
        in_specs=[pl.BlockSpec((1, lh, CONV_TILES, LANES), lambda i: (i, 0, 0, 0)),
                  whole(CONV_WIDTH), whole(1), whole(1), whole(1)],
        out_specs=pl.BlockSpec((1, length, CONV_TILES, LANES), lambda i: (i, 0, 0, 0)),
        out_shape=jax.ShapeDtypeStruct((b, length, CONV_TILES, LANES), F32),
        compiler_params=_cparams(("arbitrary",)),
        name=name,
    )(hist, tile(conv_w), tile(conv_b), tile(ln_g), tile(ln_b))


def _lambda_value(lam_ref, lambda_init):
    v = lam_ref[...]
    s1 = jnp.sum(v[0:1] * v[1:2], axis=-1, keepdims=True)
    s2 = jnp.sum(v[2:3] * v[3:4], axis=-1, keepdims=True)
    return jnp.exp(s1) - jnp.exp(s2) + lambda_init


def _prompt_attn_kernel(q_ref, k_ref, v_ref, lam_ref, sg_ref, o_ref, m_s, l_s, acc_s, *, lambda_init):
    qi, ki = pl.program_id(2), pl.program_id(3)

    @pl.when(ki == 0)
    def _():
        m_s[...] = jnp.full(m_s.shape, NEG_INF, F32)
        l_s[...] = jnp.zeros(l_s.shape, F32)
        acc_s[...] = jnp.zeros(acc_s.shape, F32)

    def update(masked):
        v = v_ref[...]
        for c in range(2):
            q = q_ref[:, c * HEAD_DIM:(c + 1) * HEAD_DIM]
            k = k_ref[:, c * HEAD_DIM:(c + 1) * HEAD_DIM]
            s = lax.dot_general(q, k, (((1,), (1,)), ((), ())), preferred_element_type=F32)
            if masked:
                row = lax.broadcasted_iota(jnp.int32, s.shape, 0)
                col = lax.broadcasted_iota(jnp.int32, s.shape, 1)
                s = jnp.where(col <= row, s, NEG_INF)
            m_old = m_s[c]
            m_new = jnp.maximum(m_old, jnp.max(s, axis=-1, keepdims=True))
            alpha = jnp.exp(m_old - m_new)
            p = jnp.exp(s - m_new)
            l_s[c] = alpha * l_s[c] + jnp.sum(p, axis=-1, keepdims=True)
            acc_s[c] = alpha * acc_s[c] + jnp.dot(p.astype(BF16), v, preferred_element_type=F32)
            m_s[c] = m_new

    @pl.when(ki < qi)
    def _():
        update(False)

    @pl.when(ki == qi)
    def _():
        update(True)
        lam = _lambda_value(lam_ref, lambda_init)
        o = acc_s[0] / l_s[0] - lam * (acc_s[1] / l_s[1])
        ms = jnp.mean(o * o, axis=-1, keepdims=True)
        o_ref[...] = (((o * lax.rsqrt(ms + EPS)) * sg_ref[...]) * (1.0 - lambda_init)).astype(o_ref.dtype)


def _prompt_attention(q, k, v, lam_rows, subln_g, batch, seq, lambda_init):
    nq = seq // TQ
    qmap = lambda b, h, i, j: (b * nq + i, h)
    kmap = lambda b, h, i, j: (b * nq + jnp.minimum(i, j), h)
    return pl.pallas_call(
        functools.partial(_prompt_attn_kernel, lambda_init=lambda_init),
        grid=(batch, N_HEADS, nq, nq),
        in_specs=[pl.BlockSpec((TQ, V_DIM), qmap), pl.BlockSpec((TQ, V_DIM), kmap), pl.BlockSpec((TQ, V_DIM), kmap),
                  pl.BlockSpec((4, HEAD_DIM), lambda b, h, i, j: (0, 0)),
                  pl.BlockSpec((1, V_DIM), lambda b, h, i, j: (0, 0))],
        out_specs=pl.BlockSpec((TQ, V_DIM), qmap),
        out_shape=jax.ShapeDtypeStruct((batch * seq, V_WIDTH), BF16),
        scratch_shapes=[pltpu.VMEM((2, TQ, 1), F32), pltpu.VMEM((2, TQ, 1), F32), pltpu.VMEM((2, TQ, V_DIM), F32)],
        compiler_params=_cparams(("arbitrary",) * 4),
        name="prompt_attention",
    )(q, k, v, lam_rows, subln_g.reshape(1, V_DIM))


def _page_tiles(ref):
    tiles = jnp.swapaxes(ref[...], 0, 1)
    return [tiles[i].astype(BF16) for i in range(SUBLANES)]


def _sample_attn_kernel(pt_ref, *refs, past, n_new, lambda_init):
    g = PAGES_PER_STEP
    k_refs, v_refs = refs[:2 * g], refs[2 * g:4 * g]
    qb_ref, kn_ref, vn_ref, lam_ref, sg_ref, o_ref, s_s, a_s, m_s, acc_s = refs[4 * g:]
    j = pl.program_id(1)
    n_pages = past // PAGE
    nk = n_pages // g
    new_rows = kn_ref.shape[1]

    def scores(kfull_lo, kfull_hi):
        half = QK_WIDTH // 2
        return (jnp.dot(kfull_lo, qb_ref[0, :half, :], preferred_element_type=F32)
                + jnp.dot(kfull_hi, qb_ref[0, half:, :], preferred_element_type=F32))

    @pl.when(j == 0)
    def _():
        m_s[...] = jnp.full(m_s.shape, NEG_INF, F32)
        acc_s[...] = jnp.zeros(acc_s.shape, F32)

    @pl.when(j < nk)
    def _():
        m = m_s[...]
        for p in range(g):
            lo = jnp.concatenate(_page_tiles(k_refs[2 * p]), axis=1)
            hi = jnp.concatenate(_page_tiles(k_refs[2 * p + 1]), axis=1)
            s = scores(lo, hi)
            row0 = pl.multiple_of((j * g + p) * PAGE, PAGE)
            s_s[pl.ds(row0, PAGE), :] = s
            m = jnp.maximum(m, jnp.max(s, axis=0, keepdims=True))
        m_s[...] = m

    @pl.when(j == nk - 1)
    def _():
        kn = kn_ref[0]
        half = QK_WIDTH // 2
        s = scores(kn[:, :half], kn[:, half:])
        key = lax.broadcasted_iota(jnp.int32, s.shape, 0)
        qry = (lax.broadcasted_iota(jnp.int32, s.shape, 1) % LANES) // SUBLANES
        s = jnp.where((key <= qry) & (key < n_new), s, NEG_INF)
        s_s[pl.ds(past, new_rows), :] = s
        m = jnp.maximum(m_s[...], jnp.max(s, axis=0, keepdims=True))
        total = past + new_rows
        chunk = max(c for c in range(SUBLANES, 513, SUBLANES) if total % c == 0)
        n_chunks = total // chunk

        def sum_step(i, l):
            r0 = pl.multiple_of(i * chunk, SUBLANES)
            p = jnp.exp(s_s[pl.ds(r0, chunk), :] - m)
            s_s[pl.ds(r0, chunk), :] = p
            return l + jnp.sum(p, axis=0, keepdims=True)

        l = lax.fori_loop(0, n_chunks, sum_step, jnp.zeros((1, 2 * LANES), F32))
        lam = _lambda_value(lam_ref, lambda_init)
        inv0 = 1.0 / l[:, :LANES]
        inv1 = lam / l[:, LANES:]

        def diff_step(i, c):
            r0 = pl.multiple_of(i * chunk, SUBLANES)
            p = s_s[pl.ds(r0, chunk), :]
            a_s[pl.ds(r0, chunk), :] = p[:, :LANES] * inv0 - p[:, LANES:] * inv1
            return c

        lax.fori_loop(0, n_chunks, diff_step, 0)

    def weights_t(row0, rows):
        a = a_s[pl.ds(row0, rows), :]
        if rows < LANES:
            a = jnp.concatenate([a, jnp.zeros((LANES - rows, LANES), F32)], axis=0)
        return a.T[:4 * SUBLANES, :rows].astype(BF16)

    @pl.when(j >= nk)
    def _():
        acc = acc_s[...]
        for p in range(g):
            tiles0 = _page_tiles(v_refs[2 * p])
            tiles1 = _page_tiles(v_refs[2 * p + 1])
            vfull = jnp.concatenate([t for pair in zip(tiles0, tiles1) for t in pair], axis=1)
            row0 = pl.multiple_of(((j - nk) * g + p) * PAGE, PAGE)
            acc = acc + jnp.dot(weights_t(row0, PAGE), vfull, preferred_element_type=F32)
        acc_s[...] = acc

    @pl.when(j == 2 * nk - 1)
    def _():
        acc = acc_s[...] + jnp.dot(weights_t(past, new_rows), vn_ref[0], preferred_element_type=F32)
        head = lax.broadcasted_iota(jnp.int32, (SUBLANES, V_WIDTH), 0)
        col_head = lax.broadcasted_iota(jnp.int32, (SUBLANES, V_WIDTH), 1) // V_DIM
        own = head == col_head
        rows = []
        for q in range(n_new):
            o = jnp.where(own, acc[q * SUBLANES:(q + 1) * SUBLANES, :], 0.0)
            ms = jnp.sum(o * o, axis=-1, keepdims=True) / V_DIM
            o = jnp.sum(o * lax.rsqrt(ms + EPS), axis=0, keepdims=True)
            rows.append((o * sg_ref[...]) * (1.0 - lambda_init))
        rows.append(jnp.zeros((SUBLANES - n_new, V_WIDTH), F32))
        o_ref[0] = jnp.concatenate(rows, axis=0)


def _sample_attention(q_blk, k_new, v_new, cache_k, cache_v, page_table, lam_rows, subln_g, lambda_init, n_new):
    nb, n_pages = page_table.shape
    past = n_pages * PAGE
    g = PAGES_PER_STEP
    nk = n_pages // g
    n_pool = cache_k.shape[0]
    ck = cache_k.reshape(n_pool * PAGE, 2, SUBLANES, HEAD_DIM)
    cv = cache_v.reshape(n_pool * PAGE, N_HEADS, V_DIM)
    pt = page_table.reshape(-1)

    def kmap(p, a):
        return lambda b, j, pt: (pt[b * n_pages + jnp.minimum(j, nk - 1) * g + p], a, 0, 0)

    def vmap(p, a):
        return lambda b, j, pt: (pt[b * n_pages + jnp.maximum(j - nk, 0) * g + p], 0, a)

    k_specs = [pl.BlockSpec((PAGE, None, SUBLANES, HEAD_DIM), kmap(p, a)) for p in range(g) for a in range(2)]
    v_specs = [pl.BlockSpec((PAGE, N_HEADS, LANES), vmap(p, a)) for p in range(g) for a in range(2)]
    new_rows = k_new.shape[1]
    per_b = lambda shape: pl.BlockSpec((1,) + shape, lambda b, j, pt: (b, 0, 0))
    const = lambda shape: pl.BlockSpec(shape, lambda b, j, pt: (0, 0))
    tiled_g = jnp.tile(subln_g.reshape(1, V_DIM), (1, N_HEADS))
    return pl.pallas_call(
        functools.partial(_sample_attn_kernel, past=past, n_new=n_new, lambda_init=lambda_init),
        grid_spec=pltpu.PrefetchScalarGridSpec(
            num_scalar_prefetch=1,
            grid=(nb, 2 * nk),
            in_specs=k_specs + v_specs + [per_b((QK_WIDTH, 2 * LANES)), per_b((new_rows, QK_WIDTH)),
                                          per_b((new_rows, V_WIDTH)), const((4, HEAD_DIM)), const((1, V_WIDTH))],
            out_specs=per_b((SUBLANES, V_WIDTH)),
            scratch_shapes=[pltpu.VMEM((past + new_rows, 2 * LANES), F32), pltpu.VMEM((past + new_rows, LANES), F32),
                            pltpu.VMEM((1, 2 * LANES), F32), pltpu.VMEM((4 * SUBLANES, V_WIDTH), F32)]),
        out_shape=jax.ShapeDtypeStruct((nb, SUBLANES, V_WIDTH), F32),
        compiler_params=_cparams(("arbitrary", "arbitrary")),
        name="sample_attention",
    )(pt, *([ck] * (2 * g)), *([cv] * (2 * g)), q_blk, k_new, v_new, lam_rows, tiled_g)


def _merge_kernel(yc_ref, ya_ref, wc_ref, wa_ref, gc_ref, ga_ref, m_ref, wc_s, wa_s):
    _cache_weight(wc_ref, wc_s)
    _cache_weight(wa_ref, wa_s)
    conv = _rows_to_tile(yc_ref, 0, ya_ref.shape[0], CONV_TILES).astype(BF16)
    yc = jnp.dot(conv, wc_s[...], preferred_element_type=F32)
    ya = jnp.dot(ya_ref[...], wa_s[...], preferred_element_type=F32)
    m_ref[...] = (gc_ref[...] * yc + ga_ref[...] * ya).astype(m_ref.dtype)


def _merge(yconv_lin, o_n, w_conv_out, w_attn_out, gates):
    t = o_n.shape[0]
    tm = TM // 2
    nj, ni = D_MODEL // TN, t // tm
    return pl.pallas_call(
        _merge_kernel,
        grid=(nj, ni),
        in_specs=[pl.BlockSpec((tm * CONV_TILES, LANES), lambda j, i: (i, 0)),
                  pl.BlockSpec((tm, V_WIDTH), lambda j, i: (i, 0)),
                  pl.BlockSpec((D_CONV, TN), lambda j, i: (0, j)), pl.BlockSpec((V_WIDTH, TN), lambda j, i: (0, j)),
                  pl.BlockSpec((tm, TN), lambda j, i: (i, j)), pl.BlockSpec((tm, TN), lambda j, i: (i, D_MODEL // TN + j))],
        out_specs=pl.BlockSpec((tm, TN), lambda j, i: (i, j)),
        out_shape=jax.ShapeDtypeStruct((t, D_MODEL), BF16),
        scratch_shapes=[pltpu.VMEM((D_CONV, TN), BF16), pltpu.VMEM((V_WIDTH, TN), BF16)],
        compiler_params=_cparams(("arbitrary", "arbitrary")),
        name="merge",
    )(yconv_lin, o_n, w_conv_out, w_attn_out, gates, gates)


def _residual_kernel(m_ref, w_ref, x_ref, o_ref, w_s):
    _cache_weight(w_ref, w_s)
    o_ref[...] = x_ref[...] + jnp.dot(m_ref[...], w_s[...], preferred_element_type=F32)


def _out_projection(m, w_out, x):
    t = m.shape[0]
    nj, ni = D_MODEL // TN, t // TM
    return pl.pallas_call(
        _residual_kernel,
        grid=(nj, ni),
        in_specs=[pl.BlockSpec((TM, D_MODEL), lambda j, i: (i, 0)), pl.BlockSpec((D_MODEL, TN), lambda j, i: (0, j)),
                  pl.BlockSpec((TM, TN), lambda j, i: (i, j))],
        out_specs=pl.BlockSpec((TM, TN), lambda j, i: (i, j)),
        out_shape=jax.ShapeDtypeStruct((t, D_MODEL), F32),
        scratch_shapes=[pltpu.VMEM((D_MODEL, TN), BF16)],
        compiler_params=_cparams(("arbitrary", "arbitrary")),
        name="out_projection",
    )(m, w_out, x)


def _router_kernel(x_ref, g_ref, w_ref, b_ref, hl_ref, e_ref, wt_ref):
    x = x_ref[...]
    tm = x.shape[0]
    ms = jnp.mean(x * x, axis=-1, keepdims=True)
    h = (x * lax.rsqrt(ms + EPS)) * g_ref[...]
    for j in range(ROW_TILES):
        hl_ref[pl.ds(j, tm, stride=ROW_TILES), :] = h[:, j * LANES:(j + 1) * LANES]
    h_hi = h.astype(BF16)
    h_lo = (h - h_hi.astype(F32)).astype(BF16)
    both = jnp.dot(h_hi, w_ref[...], preferred_element_type=F32)
    logits = both[:, :LANES] + both[:, LANES:] + jnp.dot(h_lo, w_ref[:, :LANES], preferred_element_type=F32)
    logits = logits + b_ref[...]
    lane = lax.broadcasted_iota(jnp.int32, logits.shape, 1)
    neg = -jnp.inf
    big = jnp.int32(LANES)
    lg = jnp.where(lane < N_GROUPS, logits, neg)
    eg = jnp.exp(lg - jnp.max(lg, axis=-1, keepdims=True))
    pg = eg / jnp.sum(eg, axis=-1, keepdims=True)
    p_top = jnp.max(pg, axis=-1, keepdims=True)
    g_idx = jnp.min(jnp.where(pg == p_top, lane, big), axis=-1, keepdims=True)
    in_group = (lane >= N_GROUPS) & (lane < N_GROUPS + N_EXPERTS) & (((lane - N_GROUPS) // EXPERTS_PER_GROUP) == g_idx)
    le = jnp.where(in_group, logits, neg)
    ee = jnp.exp(le - jnp.max(le, axis=-1, keepdims=True))
    pe = jnp.where(in_group, ee / jnp.sum(ee, axis=-1, keepdims=True), -1.0)
    w1 = jnp.max(pe, axis=-1, keepdims=True)
    i1 = jnp.min(jnp.where(pe == w1, lane, big), axis=-1, keepdims=True)
    pe2 = jnp.where(lane == i1, -1.0, pe)
    w2 = jnp.max(pe2, axis=-1, keepdims=True)
    i2 = jnp.min(jnp.where(pe2 == w2, lane, big), axis=-1, keepdims=True)
    denom = w1 + w2
    e_ref[...] = jnp.where(lane == 0, i1 - N_GROUPS, jnp.where(lane == 1, i2 - N_GROUPS, 0))
    wt_ref[...] = jnp.where(lane == 0, w1 / denom * p_top, jnp.where(lane == 1, w2 / denom * p_top, 0.0))


def _router(x1, norm2_g, w_router, b_router, tm):
    t = x1.shape[0]
    return pl.pallas_call(
        _router_kernel,
        grid=(t // tm,),
        in_specs=[pl.BlockSpec((tm, D_MODEL), lambda i: (i, 0)), pl.BlockSpec((1, D_MODEL), lambda i: (0, 0)),
                  pl.BlockSpec((D_MODEL, 2 * LANES), lambda i: (0, 0)), pl.BlockSpec((1, LANES), lambda i: (0, 0))],
        out_specs=[pl.BlockSpec((tm * ROW_TILES, LANES), lambda i: (i, 0)),
                   pl.BlockSpec((tm, LANES), lambda i: (i, 0)), pl.BlockSpec((tm, LANES), lambda i: (i, 0))],
        out_shape=[jax.ShapeDtypeStruct((t * ROW_TILES, LANES), F32),
                   jax.ShapeDtypeStruct((t, LANES), jnp.int32), jax.ShapeDtypeStruct((t, LANES), F32)],
        compiler_params=_cparams(("arbitrary",)),
        name="router",
    )(x1, norm2_g.reshape(1, D_MODEL), w_router, b_router)


def _row_copy(src_hbm, src_row, dst, dst_row, sem):
    return pltpu.make_async_copy(src_hbm.at[pl.ds(src_row * ROW_TILES, ROW_TILES), :],
                                 dst.at[pl.ds(dst_row * ROW_TILES, ROW_TILES), :], sem)


def _moe_kernel(te_ref, tr_ref, rp_ref, h_hbm, wg_ref, wu_ref, wd_ref, y_ref, g_s, sem, wg_s, wu_s, wd_s):
    i = pl.program_id(0)
    nt = pl.num_programs(0)
    slot = i % 2

    def issue(tile, dst_slot):
        def body(r, c):
            tok = rp_ref[tile * MOE_TM + r] // TOP_K
            _row_copy(h_hbm, tok, g_s.at[dst_slot], r, sem.at[dst_slot]).start()
            return c
        lax.fori_loop(0, tr_ref[tile], body, 0)

    @pl.when(i == 0)
    def _():
        g_s[...] = jnp.zeros(g_s.shape, F32)
        issue(0, 0)

    @pl.when(i + 1 < nt)
    def _():
        issue(i + 1, 1 - slot)

    n = tr_ref[i]

    @pl.when(n == 0)
    def _():
        y_ref[...] = jnp.zeros(y_ref.shape, F32)

    @pl.when(n > 0)
    def _():
        pltpu.make_async_copy(h_hbm.at[pl.ds(0, n * ROW_TILES), :], g_s.at[slot, pl.ds(0, n * ROW_TILES), :],
                              sem.at[slot]).wait()

        changed = jnp.logical_or(i == 0, te_ref[i] != te_ref[jnp.maximum(i - 1, 0)])

        @pl.when(changed)
        def _():
            wg_s[...] = wg_ref[0].astype(BF16)
            wu_s[...] = wu_ref[0].astype(BF16)
            wd_s[...] = wd_ref[0].astype(BF16)

        x = _rows_to_tile(g_s.at[slot], 0, MOE_TM, ROW_TILES).astype(BF16)
        hg = jnp.dot(x, wg_s[...], preferred_element_type=F32)
        hu = jnp.dot(x, wu_s[...], preferred_element_type=F32)
        act = ((hg * jax.nn.sigmoid(hg)) * hu).astype(BF16)
        y = jnp.dot(act, wd_s[...], preferred_element_type=F32)
        for j in range(ROW_TILES):
            y_ref[pl.ds(j, MOE_TM, stride=ROW_TILES), :] = y[:, j * LANES:(j + 1) * LANES]


def _moe(h_lin, tile_expert, tile_rows, row_pair, w_gate, w_up, w_down, n_tiles):
    wmap = lambda i, te, tr, rp: (te[i], 0, 0)
    return pl.pallas_call(
        _moe_kernel,
        grid_spec=pltpu.PrefetchScalarGridSpec(
            num_scalar_prefetch=3,
            grid=(n_tiles,),
            in_specs=[pl.BlockSpec(memory_space=pl.ANY),
                      pl.BlockSpec((1, D_MODEL, D_FF), wmap), pl.BlockSpec((1, D_MODEL, D_FF), wmap),
                      pl.BlockSpec((1, D_FF, D_MODEL), wmap)],
            out_specs=pl.BlockSpec((MOE_TM * ROW_TILES, LANES), lambda i, te, tr, rp: (i, 0)),
            scratch_shapes=[pltpu.VMEM((2, MOE_TM * ROW_TILES, LANES), F32), pltpu.SemaphoreType.DMA((2,)),
                            pltpu.VMEM((D_MODEL, D_FF), BF16), pltpu.VMEM((D_MODEL, D_FF), BF16),
                            pltpu.VMEM((D_FF, D_MODEL), BF16)]),
        out_shape=jax.ShapeDtypeStruct((n_tiles * MOE_TM * ROW_TILES, LANES), F32),
        compiler_params=_cparams(("arbitrary",)),
        name="moe_experts",
    )(tile_expert, tile_rows, row_pair, h_lin, w_gate, w_up, w_down)


def _combine_kernel(dr_ref, ys_hbm, x_ref, wt_ref, o_ref, g_s, sem):
    i = pl.program_id(0)
    nt = pl.num_programs(0)
    slot = i % 2

    def issue(tile, dst_slot):
        def body(r, c):
            for k in range(TOP_K):
                row = dr_ref[(tile * CMB_TM + r) * TOP_K + k]
                _row_copy(ys_hbm, row, g_s.at[dst_slot], k * CMB_TM + r, sem.at[dst_slot]).start()
            return c
        lax.fori_loop(0, CMB_TM, body, 0)

    @pl.when(i == 0)
    def _():
        issue(0, 0)

    @pl.when(i + 1 < nt)
    def _():
        issue(i + 1, 1 - slot)

    pltpu.make_async_copy(ys_hbm.at[pl.ds(0, TOP_K * CMB_TM * ROW_TILES), :], g_s.at[slot], sem.at[slot]).wait()
    wt = wt_ref[...]
    y0 = _rows_to_tile(g_s.at[slot], 0, CMB_TM, ROW_TILES)
    y1 = _rows_to_tile(g_s.at[slot], CMB_TM * ROW_TILES, CMB_TM, ROW_TILES)
    o_ref[...] = x_ref[...] + (wt[:, 0:1] * y0 + wt[:, 1:2] * y1)


def _combine(dest_row, y_sorted, x1, wt):
    t = x1.shape[0]
    tile = lambda i, dr: (i, 0)
    return pl.pallas_call(
        _combine_kernel,
        grid_spec=pltpu.PrefetchScalarGridSpec(
            num_scalar_prefetch=1,
            grid=(t // CMB_TM,),
            in_specs=[pl.BlockSpec(memory_space=pl.ANY), pl.BlockSpec((CMB_TM, D_MODEL), tile),
                      pl.BlockSpec((CMB_TM, LANES), tile)],
            out_specs=pl.BlockSpec((CMB_TM, D_MODEL), tile),
            scratch_shapes=[pltpu.VMEM((2, TOP_K * CMB_TM * ROW_TILES, LANES), F32), pltpu.SemaphoreType.DMA((2,))]),
        out_shape=jax.ShapeDtypeStruct((t, D_MODEL), F32),
        compiler_params=_cparams(("arbitrary",)),
        name="moe_combine",
    )(dest_row, y_sorted, x1, wt)


def _routing_plan(expert_idx, n_tiles):
    e_flat = expert_idx.reshape(-1)
    n_pairs = e_flat.shape[0]
    order = jnp.argsort(e_flat, stable=True).astype(jnp.int32)
    counts = jnp.sum(jax.nn.one_hot(e_flat, N_EXPERTS, dtype=jnp.int32), axis=0)
    tiles_e = (counts + MOE_TM - 1) // MOE_TM
    tile_end = jnp.cumsum(tiles_e)
    tile_start = tile_end - tiles_e
    sorted_start = jnp.cumsum(counts) - counts
    e_sorted = e_flat[order]
    padded_row = tile_start[e_sorted] * MOE_TM + (jnp.arange(n_pairs, dtype=jnp.int32) - sorted_start[e_sorted])
    row_pair = jnp.zeros((n_tiles * MOE_TM,), jnp.int32).at[padded_row].set(order)
    dest_row = jnp.zeros((n_pairs,), jnp.int32).at[order].set(padded_row)
    n_used = tile_end[-1]
    tile_id = jnp.arange(n_tiles, dtype=jnp.int32)
    te = jnp.searchsorted(tile_end, jnp.minimum(tile_id, n_used - 1), side="right").astype(jnp.int32)
    te = jnp.minimum(te, N_EXPERTS - 1)
    tile_rows = jnp.where(tile_id < n_used, jnp.clip(counts[te] - (tile_id - tile_start[te]) * MOE_TM, 0, MOE_TM), 0)
    return te, tile_rows.astype(jnp.int32), row_pair, dest_row.astype(jnp.int32)


def _rotary_tables(pos):
    half = ROT_DIM // 2
    inv_freq = ROPE_THETA ** (-jnp.arange(0, ROT_DIM, 2, dtype=F32) / ROT_DIM)
    ang = pos[:, None] * inv_freq[None, :]
    cos, sin = jnp.cos(ang), jnp.sin(ang)
    rest = LANES - ROT_DIM
    ones = jnp.ones((pos.shape[0], rest), F32)
    zeros = jnp.zeros((pos.shape[0], rest), F32)
    return jnp.concatenate([cos, cos, ones], axis=1), jnp.concatenate([-sin, sin, zeros], axis=1)


def _query_blocks(q_s):
    n = jnp.arange(QK_WIDTH) // HEAD_DIM
    col = jnp.arange(2 * LANES)
    col_q = (col % LANES) // SUBLANES
    keep = ((n[:, None] // 2 == col[None, :] % SUBLANES) & (n[:, None] % 2 == col[None, :] // LANES)
            & ((col % LANES) < q_s.shape[1] * SUBLANES)[None, :])
    picked = jnp.swapaxes(q_s[:, jnp.minimum(col_q, q_s.shape[1] - 1), :], 1, 2)
    return jnp.where(keep[None], picked, jnp.zeros((), q_s.dtype))


def _layer(x_p, x_s, cache_k, cache_v, state_conv, page_table, lambda_init,
           norm1_g, w_in, conv_w, conv_b, conv_ln_g, conv_ln_b, w_conv_out, q_norm_g, k_norm_g,
           lambda_q1, lambda_k1, lambda_q2, lambda_k2, subln_g, w_attn_out, w_out, norm2_g,
           w_router_group, b_router_group, w_router_expert, b_router_expert, w_gate_e, w_up_e, w_down_e):
    batch, seq, _ = x_p.shape
    nb, n_new, _ = x_s.shape
    n_pages = page_table.shape[1]
    past = n_pages * PAGE
    tp, ts = batch * seq, nb * n_new
    t = tp + ts
    x = jnp.concatenate([x_p.reshape(tp, D_MODEL), x_s.reshape(ts, D_MODEL)], axis=0)

    h = _rmsnorm(x, norm1_g, 640)
    pos = jnp.concatenate([jnp.tile(jnp.arange(seq, dtype=F32), batch),
                           jnp.tile(past + jnp.arange(n_new, dtype=F32), nb)])
    tcos, tsin = _rotary_tables(pos)
    rot_specs = [pl.BlockSpec((1, LANES), lambda j, i: (0, 0)),
                 pl.BlockSpec((TM, LANES), lambda j, i: (i, 0)), pl.BlockSpec((TM, LANES), lambda j, i: (i, 0))]
    c_q = 2 * D_CONV
    c_k = c_q + QK_WIDTH
    c_v = c_k + QK_WIDTH
    c_g = c_v + V_WIDTH
    u_lin = _glu_call(h, w_in, "proj_glu")
    (q_b,) = _proj_call(_q_kernel, h, w_in, c_q, QK_WIDTH, [q_norm_g.reshape(1, LANES), tcos, tsin], rot_specs,
                        [BF16], "proj_q")
    k_f, k_b = _proj_call(_k_kernel, h, w_in, c_k, QK_WIDTH, [k_norm_g.reshape(1, LANES), tcos, tsin], rot_specs,
                          [F32, BF16], "proj_k")
    v_f, v_b = _proj_call(_v_kernel, h, w_in, c_v, V_WIDTH, [], [], [F32, BF16], "proj_v")
    (gates,) = _proj_call(_gate_kernel, h, w_in, c_g, 2 * D_MODEL, [], [], [F32], "proj_gates")

    u = u_lin.reshape(t, CONV_TILES, LANES)
    u_p = u[:tp].reshape(batch, seq, CONV_TILES, LANES)
    u_s = u[tp:].reshape(nb, n_new, CONV_TILES, LANES)
    hist_p = jnp.concatenate([jnp.zeros((batch, HIST, CONV_TILES, LANES), F32), u_p], axis=1)
    hist_s = jnp.concatenate([state_conv.reshape(nb, HIST, CONV_TILES, LANES), u_s], axis=1)
    yc_p = _conv_branch(hist_p, conv_w, conv_b, conv_ln_g, conv_ln_b, "conv_prompt")
    yc_s = _conv_branch(hist_s, conv_w, conv_b, conv_ln_g, conv_ln_b, "conv_sample")
    yconv = jnp.concatenate([yc_p.reshape(tp * CONV_TILES, LANES), yc_s.reshape(ts * CONV_TILES, LANES)], axis=0)
    new_conv_p = hist_p[:, seq:].reshape(batch, HIST, D_CONV)
    new_conv_s = hist_s[:, n_new:].reshape(nb, HIST, D_CONV)

    lam_rows = jnp.stack([lambda_q1, lambda_k1, lambda_q2, lambda_k2]).astype(F32)
    o_p = _prompt_attention(q_b, k_b, v_b, lam_rows, subln_g, batch, seq, lambda_init)
    pad_rows = lambda a: jnp.pad(a.reshape(nb, n_new, -1), ((0, 0), (0, NEW_ROWS - n_new), (0, 0)))
    q_blk = _query_blocks(q_b[tp:].reshape(nb, n_new, QK_WIDTH))
    o_s = _sample_attention(q_blk, pad_rows(k_b[tp:]), pad_rows(v_b[tp:]), cache_k, cache_v, page_table,
                            lam_rows, subln_g, lambda_init, n_new)
    o_n = jnp.concatenate([o_p, o_s[:, :n_new].reshape(ts, V_WIDTH).astype(BF16)], axis=0)

    m = _merge(yconv, o_n, w_conv_out, w_attn_out, gates)
    x1 = _out_projection(m, w_out, x)

    w_r = jnp.zeros((D_MODEL, LANES), F32).at[:, :N_GROUPS].set(w_router_group)
    w_r = w_r.at[:, N_GROUPS:N_GROUPS + N_EXPERTS].set(w_router_expert)
    w_hi = w_r.astype(BF16)
    w_lo = (w_r - w_hi.astype(F32)).astype(BF16)
    b_r = jnp.zeros((1, LANES), F32).at[0, :N_GROUPS].set(b_router_group)
    b_r = b_r.at[0, N_GROUPS:N_GROUPS + N_EXPERTS].set(b_router_expert)
    h_lin, e_idx, wt = _router(x1, norm2_g, jnp.concatenate([w_hi, w_lo], axis=1), b_r, 640)
    n_tiles = -(-t * TOP_K // MOE_TM) + N_EXPERTS
    te, tile_rows, row_pair, dest_row = _routing_plan(e_idx[:, :TOP_K], n_tiles)
    y_sorted = _moe(h_lin, te, tile_rows, row_pair, w_gate_e, w_up_e, w_down_e, n_tiles)
    y = _combine(dest_row, y_sorted, x1, wt)

    y_p = y[:tp].reshape(batch, seq, D_MODEL)
    y_s = y[tp:].reshape(nb, n_new, D_MODEL)
    k_p = k_f[:tp].reshape(batch, seq, N_HEADS, 2, HEAD_DIM)
    k_s = k_f[tp:].reshape(nb, n_new, N_HEADS, 2, HEAD_DIM)
    v_p = v_f[:tp].reshape(batch, seq, N_HEADS, V_DIM)
    v_s = v_f[tp:].reshape(nb, n_new, N_HEADS, V_DIM)
    return y_p, y_s, k_p, v_p, new_conv_p, k_s, v_s, new_conv_s


def kernel(x_prompt, x_sample, cache_k, cache_v, state_conv, page_table, norm1_g, w_in, conv_w, conv_b, conv_ln_g, conv_ln_b, w_conv_out, q_norm_g, k_norm_g, lambda_q1, lambda_k1, lambda_q2, lambda_k2, subln_g, w_attn_out, w_out, norm2_g, w_router_group, b_router_group, w_router_expert, b_router_expert, w_gate_e, w_up_e, w_down_e):
    depth = norm1_g.shape[0]
    x_p, x_s = x_prompt, x_sample
    outs = [[] for _ in range(6)]
    for l in range(depth):
        lambda_init = 0.8 - 0.6 * math.exp(-0.3 * l)
        x_p, x_s, *rest = _layer(
            x_p, x_s, cache_k[l], cache_v[l], state_conv[l], page_table, lambda_init,
            norm1_g[l], w_in[l], conv_w[l], conv_b[l], conv_ln_g[l], conv_ln_b[l], w_conv_out[l],
            q_norm_g[l], k_norm_g[l], lambda_q1[l], lambda_k1[l], lambda_q2[l], lambda_k2[l], subln_g[l],
            w_attn_out[l], w_out[l], norm2_g[l], w_router_group[l], b_router_group[l],
            w_router_expert[l], b_router_expert[l], w_gate_e[l], w_up_e[l], w_down_e[l])
        for acc, val in zip(outs, rest):
            acc.append(val)
    return (x_p, x_s) + tuple(jnp.stack(o) for o in outs)
```

```python
import functools
import math

import jax
import jax.numpy as jnp
from jax import lax
from jax.experimental import pallas as pl
from jax.experimental.pallas import tpu as pltpu

F32 = jnp.float32
BF16 = jnp.bfloat16

D_MODEL = 2048
N_HEADS = 8
HEAD_DIM = 128
V_DIM = 2 * HEAD_DIM
QK_WIDTH = N_HEADS * 2 * HEAD_DIM
V_WIDTH = N_HEADS * V_DIM
ROT_DIM = HEAD_DIM // 4
ROPE_THETA = 500000.0
ATTN_SCALE = HEAD_DIM ** -0.5
LOG2E = math.log2(math.e)
NEG_INF = -1e30
D_CONV = D_MODEL // 2
CONV_WIDTH = 31
HIST = CONV_WIDTH - 1
N_GROUPS = 4
EXPERTS_PER_GROUP = 8
N_EXPERTS = N_GROUPS * EXPERTS_PER_GROUP
TOP_K = 2
D_FF = D_MODEL // 4
EPS = 1e-6
PAGE = 128

LANES = 128
SUBLANES = 8
ROW_TILES = D_MODEL // LANES
CONV_TILES = D_CONV // LANES
VMEM_LIMIT = 56 * 1024 * 1024

TM_MAX = 1024
TN = 512
ATTN_ROWS = 256
PAGES_PER_STEP = 4
QROWS = 2 * 4 * SUBLANES
NEW_ROWS = 16
MOE_TM = 256
CMB_TM = 256
DISPATCH_TOKENS = 64
CONV_ROWS = 16


def _cparams(sem):
    return pltpu.CompilerParams(dimension_semantics=sem, vmem_limit_bytes=VMEM_LIMIT)


def _rows_to_tile(buf, base, rows, tiles):
    return jnp.concatenate([buf[pl.ds(base + c, rows, stride=tiles), :] for c in range(tiles)], axis=1)


def _rmsnorm_kernel(x_ref, g_ref, o_ref):
    x = x_ref[...]
    ms = jnp.mean(x * x, axis=-1, keepdims=True)
    o_ref[...] = ((x * lax.rsqrt(ms + EPS)) * g_ref[...]).astype(o_ref.dtype)


def _rmsnorm(x, g, tm):
    t, d = x.shape
    return pl.pallas_call(
        _rmsnorm_kernel,
        grid=(t // tm,),
        in_specs=[pl.BlockSpec((tm, d), lambda i: (i, 0)), pl.BlockSpec((1, d), lambda i: (0, 0))],
        out_specs=pl.BlockSpec((tm, d), lambda i: (i, 0)),
        out_shape=jax.ShapeDtypeStruct((t, d), BF16),
        compiler_params=_cparams(("arbitrary",)),
        name="rmsnorm1",
    )(x, g.reshape(1, d))


def _cache_weight(w_ref, w_s):
    @pl.when(pl.program_id(1) == 0)
    def _():
        w_s[...] = w_ref[...].astype(BF16)


def _store_row_linear(o_ref, val, first_tile, tiles, row_of_tile=None):
    rows = val.shape[0]
    for c in range(val.shape[1] // LANES):
        tile = first_tile + c
        dst = tile if row_of_tile is None else row_of_tile(tile)
        o_ref[pl.ds(dst, rows, stride=tiles), :] = val[:, c * LANES:(c + 1) * LANES]


def _glu_kernel(h_ref, wa_ref, wb_ref, u_ref):
    h = h_ref[...]
    a = jnp.dot(h, wa_ref[...].astype(BF16), preferred_element_type=F32)
    b = jnp.dot(h, wb_ref[...].astype(BF16), preferred_element_type=F32)
    _store_row_linear(u_ref, a * jax.nn.sigmoid(b), pl.program_id(1) * (TN // LANES), CONV_TILES)


def _glu_call(h, w, tm):
    t, k = h.shape
    nj = D_CONV // TN
    return pl.pallas_call(
        _glu_kernel,
        grid=(t // tm, nj),
        in_specs=[pl.BlockSpec((tm, k), lambda i, j: (i, 0)), pl.BlockSpec((k, TN), lambda i, j: (0, j)),
                  pl.BlockSpec((k, TN), lambda i, j: (0, nj + j))],
        out_specs=pl.BlockSpec((tm * CONV_TILES, LANES), lambda i, j: (i, 0)),
        out_shape=jax.ShapeDtypeStruct((t * CONV_TILES, LANES), F32),
        compiler_params=_cparams(("arbitrary", "arbitrary")),
        name="proj_glu",
    )(h, w, w)


def _head_norm_rotary(acc, gain, tcos, tsin):
    lane = lax.broadcasted_iota(jnp.int32, (acc.shape[0], LANES), 1)
    half = ROT_DIM // 2
    outs = []
    for g in range(acc.shape[1] // LANES):
        t = acc[:, g * LANES:(g + 1) * LANES]
        ms = jnp.mean(t * t, axis=-1, keepdims=True)
        t = (t * lax.rsqrt(ms + EPS)) * gain
        partner = jnp.where(lane < half, pltpu.roll(t, LANES - half, 1), pltpu.roll(t, half, 1))
        outs.append(t * tcos + partner * tsin)
    return jnp.concatenate(outs, axis=1)


def _q_kernel(h_ref, w_ref, g_ref, tc_ref, ts_ref, ob_ref, w_s):
    _cache_weight(w_ref, w_s)
    acc = jnp.dot(h_ref[...], w_s[...], preferred_element_type=F32)
    y = _head_norm_rotary(acc, g_ref[...], tc_ref[...], ts_ref[...])
    ob_ref[...] = (y * (ATTN_SCALE * LOG2E)).astype(BF16)


def _gate_kernel(h_ref, w_ref, o_ref, w_s):
    _cache_weight(w_ref, w_s)
    o_ref[...] = jax.nn.sigmoid(jnp.dot(h_ref[...], w_s[...], preferred_element_type=F32))


def _stationary_call(kernel, h, w, col0, width, extra_in, extra_specs, out_dtype, tm, name):
    t, k = h.shape
    return pl.pallas_call(
        kernel,
        grid=(width // TN, t // tm),
        in_specs=[pl.BlockSpec((tm, k), lambda j, i: (i, 0)),
                  pl.BlockSpec((k, TN), lambda j, i: (0, col0 // TN + j))] + extra_specs,
        out_specs=pl.BlockSpec((tm, TN), lambda j, i: (i, j)),
        out_shape=jax.ShapeDtypeStruct((t, width), out_dtype),
        scratch_shapes=[pltpu.VMEM((k, TN), BF16)],
        compiler_params=_cparams(("arbitrary", "arbitrary")),
        name=name,
    )(h, w, *extra_in)


def _k_kernel(h_ref, w_ref, g_ref, tc_ref, ts_ref, of_ref, ob_ref):
    acc = jnp.dot(h_ref[...], w_ref[...].astype(BF16), preferred_element_type=F32)
    y = _head_norm_rotary(acc, g_ref[...], tc_ref[...], ts_ref[...])
    _store_row_linear(of_ref, y, pl.program_id(1) * (TN // LANES), ROW_TILES)
    ob_ref[...] = y.astype(BF16)


def _v_kernel(h_ref, w_ref, of_ref, ob_ref):
    acc = jnp.dot(h_ref[...], w_ref[...].astype(BF16), preferred_element_type=F32)
    _store_row_linear(of_ref, acc, pl.program_id(1) * (TN // LANES), ROW_TILES,
                      row_of_tile=lambda tile: (tile % 2) * N_HEADS + tile // 2)
    ob_ref[...] = acc.astype(BF16)


def _row_linear_call(kernel, h, w, col0, width, extra_in, extra_specs, tm, name):
    t, k = h.shape
    tiles = width // LANES
    return pl.pallas_call(
        kernel,
        grid=(t // tm, width // TN),
        in_specs=[pl.BlockSpec((tm, k), lambda i, j: (i, 0)),
                  pl.BlockSpec((k, TN), lambda i, j: (0, col0 // TN + j))] + extra_specs,
        out_specs=[pl.BlockSpec((tm * tiles, LANES), lambda i, j: (i, 0)), pl.BlockSpec((tm, TN), lambda i, j: (i, j))],
        out_shape=[jax.ShapeDtypeStruct((t * tiles, LANES), F32), jax.ShapeDtypeStruct((t, width), BF16)],
        compiler_params=_cparams(("arbitrary", "arbitrary")),
        name=name,
    )(h, w, *extra_in)


def _conv_kernel(hist_ref, w_ref, b_ref, g_ref, beta_ref, o_ref, *, length):
    rows = min(CONV_ROWS, length)

    def step(i, carry):
        t0 = i * rows
        acc = jnp.zeros((rows, CONV_TILES, LANES), F32)
        for j in range(CONV_WIDTH):
            acc = acc + hist_ref[0, pl.ds(t0 + j, rows)] * w_ref[j]
        y = acc + b_ref[...]
        mu = jnp.mean(y, axis=(1, 2), keepdims=True)
        yc = y - mu
        var = jnp.mean(yc * yc, axis=(1, 2), keepdims=True)
        z = yc * lax.rsqrt(var + EPS) * g_ref[...] + beta_ref[...]
        o_ref[0, pl.ds(t0, rows)] = z * jax.nn.sigmoid(z)
        return carry

    lax.fori_loop(0, length // rows, step, 0)


def _conv_branch(hist, conv_w, conv_b, ln_g, ln_b, name):
    b, lh = hist.shape[:2]
    length = lh - HIST
    tile = lambda a: a.reshape(-1, CONV_TILES, LANES)
    whole = lambda n: pl.BlockSpec((n, CONV_TILES, LANES), lambda i: (0, 0, 0))
    return pl.pallas_call(
        functools.partial(_conv_kernel, length=length),
        grid=(b,),
        in_specs=[pl.BlockSpec((1, lh, CONV_TILES, LANES), lambda i: (i, 0, 0, 0)),
                  whole(CONV_WIDTH), whole(1), whole(1), whole(1)],
        out_specs=pl.BlockSpec((1, length, CONV_TILES, LANES), lambda i: (i, 0, 0, 0)),
        out_shape=jax.ShapeDtypeStruct((b, length, CONV_TILES, LANES), F32),
        compiler_params=_cparams(("arbitrary",)),
        name=name,
    )(hist, tile(conv_w), tile(conv_b), tile(ln_g), tile(ln_b))


def _lambda_value(lam_ref, lambda_init):
    v = lam_ref[...]
    s1 = jnp.sum(v[0:1] * v[1:2], axis=-1, keepdims=True)
    s2 = jnp.sum(v[2:3] * v[3:4], axis=-1, keepdims=True)
    return jnp.exp(s1) - jnp.exp(s2) + lambda_init


def _prompt_attn_kernel(q_ref, k_ref, v_ref, lam_ref, sg_ref, o_ref, *, lambda_init, seq):
    rb = ATTN_ROWS
    lam = _lambda_value(lam_ref, lambda_init)
    row = lax.broadcasted_iota(jnp.int32, (rb, rb), 0)
    col = lax.broadcasted_iota(jnp.int32, (rb, rb), 1)
    for r in range(seq // rb):
        rows = pl.ds(r * rb, rb)
        kc = (r + 1) * rb
        v = v_ref[0:kc, :]
        outs = []
        for c in range(2):
            q = q_ref[rows, c * HEAD_DIM:(c + 1) * HEAD_DIM]
            k = k_ref[0:kc, c * HEAD_DIM:(c + 1) * HEAD_DIM]
            s = lax.dot_general(q, k, (((1,), (1,)), ((), ())), preferred_element_type=F32)
            diag = jnp.where(col <= row, s[:, kc - rb:], NEG_INF)
            s = diag if r == 0 else jnp.concatenate([s[:, :kc - rb], diag], axis=1)
            p = jnp.exp2(s - jnp.max(s, axis=-1, keepdims=True))
            l = jnp.sum(p, axis=-1, keepdims=True)
            outs.append(jnp.dot(p.astype(BF16), v, preferred_element_type=F32) / l)
        o = outs[0] - lam * outs[1]
        ms = jnp.mean(o * o, axis=-1, keepdims=True)
        o_ref[rows, :] = (((o * lax.rsqrt(ms + EPS)) * sg_ref[...]) * (1.0 - lambda_init)).astype(o_ref.dtype)


def _prompt_attention(q, k, v, lam_rows, subln_g, batch, seq, lambda_init):
    blk = lambda b, h: (b, h)
    return pl.pallas_call(
        functools.partial(_prompt_attn_kernel, lambda_init=lambda_init, seq=seq),
        grid=(batch, N_HEADS),
        in_specs=[pl.BlockSpec((seq, V_DIM), blk), pl.BlockSpec((seq, V_DIM), blk), pl.BlockSpec((seq, V_DIM), blk),
                  pl.BlockSpec((4, HEAD_DIM), lambda b, h: (0, 0)), pl.BlockSpec((1, V_DIM), lambda b, h: (0, 0))],
        out_specs=pl.BlockSpec((seq, V_DIM), blk),
        out_shape=jax.ShapeDtypeStruct((batch * seq, V_WIDTH), BF16),
        compiler_params=_cparams(("arbitrary", "arbitrary")),
        name="prompt_attention",
    )(q, k, v, lam_rows, subln_g.reshape(1, V_DIM))


def _page_tiles(ref):
    tiles = jnp.swapaxes(ref[...], 0, 1)
    return [tiles[i].astype(BF16) for i in range(SUBLANES)]


def _sample_attn_kernel(pt_ref, *refs, n_new, lambda_init):
    g = PAGES_PER_STEP
    k_refs, v_refs = refs[:2 * g], refs[2 * g:4 * g]
    q_ref, kn_ref, vn_ref, lam_ref, sg_ref, o_ref, m_s, l_s, acc_s = refs[4 * g:]
    j = pl.program_id(1)
    q2 = q_ref[0]
    nt_dims = (((1,), (1,)), ((), ()))

    @pl.when(j == 0)
    def _():
        m_s[...] = jnp.full(m_s.shape, NEG_INF, F32)
        l_s[...] = jnp.zeros(l_s.shape, F32)
        acc_s[...] = jnp.zeros(acc_s.shape, F32)

    def online_update(s, values):
        m_old = m_s[...]
        m_new = jnp.maximum(m_old, jnp.max(s, axis=-1, keepdims=True))
        alpha = jnp.exp2(m_old - m_new)
        p = jnp.exp2(s - m_new)
        l_s[...] = alpha * l_s[...] + jnp.sum(p, axis=-1, keepdims=True)
        pv = None
        c0 = 0
        for n, v in values:
            part = jnp.dot(p[:, c0:c0 + n].astype(BF16), v, preferred_element_type=F32)
            pv = part if pv is None else pv + part
            c0 += n
        acc_s[...] = alpha * acc_s[...] + pv
        m_s[...] = m_new

    s_parts, values = [], []
    for p in range(g):
        kfull = jnp.concatenate(_page_tiles(k_refs[2 * p]) + _page_tiles(k_refs[2 * p + 1]), axis=1)
        s_parts.append(lax.dot_general(q2, kfull, nt_dims, preferred_element_type=F32))
        t0, t1 = _page_tiles(v_refs[2 * p]), _page_tiles(v_refs[2 * p + 1])
        values.append((PAGE, jnp.concatenate([t for pair in zip(t0, t1) for t in pair], axis=1)))
    online_update(jnp.concatenate(s_parts, axis=1), values)

    @pl.when(j == pl.num_programs(1) - 1)
    def _():
        kn = kn_ref[0]
        s = lax.dot_general(q2, kn, nt_dims, preferred_element_type=F32)
        key = lax.broadcasted_iota(jnp.int32, s.shape, 1)
        qry = (lax.broadcasted_iota(jnp.int32, s.shape, 0) % (4 * SUBLANES)) // SUBLANES
        s = jnp.where((key <= qry) & (key < n_new), s, NEG_INF)
        online_update(s, [(kn.shape[0], vn_ref[0])])
        lam = _lambda_value(lam_ref, lambda_init)
        half = QROWS // 2
        acc, l = acc_s[...], l_s[...]
        o_all = acc[:half] / l[:half] - lam * (acc[half:] / l[half:])
        head = lax.broadcasted_iota(jnp.int32, (SUBLANES, V_WIDTH), 0)
        own = head == lax.broadcasted_iota(jnp.int32, (SUBLANES, V_WIDTH), 1) // V_DIM
        rows = []
        for q in range(n_new):
            o = jnp.where(own, o_all[q * SUBLANES:(q + 1) * SUBLANES, :], 0.0)
            ms = jnp.sum(o * o, axis=-1, keepdims=True) / V_DIM
            o = jnp.sum(o * lax.rsqrt(ms + EPS), axis=0, keepdims=True)
            rows.append((o * sg_ref[...]) * (1.0 - lambda_init))
        rows.append(jnp.zeros((SUBLANES - n_new, V_WIDTH), F32))
        o_ref[0] = jnp.concatenate(rows, axis=0)


def _sample_attention(q_rows, k_new, v_new, cache_k, cache_v, page_table, lam_rows, subln_g, lambda_init, n_new):
    nb, n_pages = page_table.shape
    g = PAGES_PER_STEP
    n_pool = cache_k.shape[0]
    ck = cache_k.reshape(n_pool * PAGE, 2, SUBLANES, HEAD_DIM)
    cv = cache_v.reshape(n_pool * PAGE, N_HEADS, V_DIM)
    pt = page_table.reshape(-1)
    kmap = lambda p, a: (lambda b, j, pt: (pt[b * n_pages + j * g + p], a, 0, 0))
    vmap = lambda p, a: (lambda b, j, pt: (pt[b * n_pages + j * g + p], 0, a))
    k_specs = [pl.BlockSpec((PAGE, None, SUBLANES, HEAD_DIM), kmap(p, a)) for p in range(g) for a in range(2)]
    v_specs = [pl.BlockSpec((PAGE, N_HEADS, LANES), vmap(p, a)) for p in range(g) for a in range(2)]
    new_rows = k_new.shape[1]
    per_b = lambda shape: pl.BlockSpec((1,) + shape, lambda b, j, pt: (b, 0, 0))
    const = lambda shape: pl.BlockSpec(shape, lambda b, j, pt: (0, 0))
    tiled_g = jnp.tile(subln_g.reshape(1, V_DIM), (1, N_HEADS))
    return pl.pallas_call(
        functools.partial(_sample_attn_kernel, n_new=n_new, lambda_init=lambda_init),
        grid_spec=pltpu.PrefetchScalarGridSpec(
            num_scalar_prefetch=1,
            grid=(nb, n_pages // g),
            in_specs=k_specs + v_specs + [per_b((QROWS, QK_WIDTH)), per_b((new_rows, QK_WIDTH)),
                                          per_b((new_rows, V_WIDTH)), const((4, HEAD_DIM)), const((1, V_WIDTH))],
            out_specs=per_b((SUBLANES, V_WIDTH)),
            scratch_shapes=[pltpu.VMEM((QROWS, 1), F32), pltpu.VMEM((QROWS, 1), F32), pltpu.VMEM((QROWS, V_WIDTH), F32)]),
        out_shape=jax.ShapeDtypeStruct((nb, SUBLANES, V_WIDTH), F32),
        compiler_params=_cparams(("arbitrary", "arbitrary")),
        name="sample_attention",
    )(pt, *([ck] * (2 * g)), *([cv] * (2 * g)), q_rows, k_new, v_new, lam_rows, tiled_g)


def _query_rows(q_s):
    n_new = q_s.shape[1]
    r = jnp.arange(QROWS)
    r_c, r_q, r_h = r // (4 * SUBLANES), (r % (4 * SUBLANES)) // SUBLANES, r % SUBLANES
    col_hc = jnp.arange(QK_WIDTH) // HEAD_DIM
    keep = (col_hc[None, :] == (r_h * 2 + r_c)[:, None]) & (r_q < n_new)[:, None]
    picked = q_s[:, jnp.minimum(r_q, n_new - 1), :]
    return jnp.where(keep[None], picked, jnp.zeros((), q_s.dtype))


def _merge_kernel(yc_ref, ya_ref, wc_ref, wa_ref, gc_ref, ga_ref, m_ref, wc_s, wa_s):
    _cache_weight(wc_ref, wc_s)
    _cache_weight(wa_ref, wa_s)
    conv = _rows_to_tile(yc_ref, 0, ya_ref.shape[0], CONV_TILES).astype(BF16)
    yc = jnp.dot(conv, wc_s[...], preferred_element_type=F32)
    ya = jnp.dot(ya_ref[...], wa_s[...], preferred_element_type=F32)
    m_ref[...] = (gc_ref[...] * yc + ga_ref[...] * ya).astype(m_ref.dtype)


def _merge(yconv_lin, o_n, w_conv_out, w_attn_out, gates, tm):
    t = o_n.shape[0]
    nj, ni = D_MODEL // TN, t // tm
    return pl.pallas_call(
        _merge_kernel,
        grid=(nj, ni),
        in_specs=[pl.BlockSpec((tm * CONV_TILES, LANES), lambda j, i: (i, 0)),
                  pl.BlockSpec((tm, V_WIDTH), lambda j, i: (i, 0)),
                  pl.BlockSpec((D_CONV, TN), lambda j, i: (0, j)), pl.BlockSpec((V_WIDTH, TN), lambda j, i: (0, j)),
                  pl.BlockSpec((tm, TN), lambda j, i: (i, j)), pl.BlockSpec((tm, TN), lambda j, i: (i, D_MODEL // TN + j))],
        out_specs=pl.BlockSpec((tm, TN), lambda j, i: (i, j)),
        out_shape=jax.ShapeDtypeStruct((t, D_MODEL), BF16),
        scratch_shapes=[pltpu.VMEM((D_CONV, TN), BF16), pltpu.VMEM((V_WIDTH, TN), BF16)],
        compiler_params=_cparams(("arbitrary", "arbitrary")),
        name="merge",
    )(yconv_lin, o_n, w_conv_out, w_attn_out, gates, gates)


def _residual_kernel(m_ref, w_ref, x_ref, o_ref, w_s):
    _cache_weight(w_ref, w_s)
    o_ref[...] = x_ref[...] + jnp.dot(m_ref[...], w_s[...], preferred_element_type=F32)


def _out_projection(m, w_out, x, tm):
    t = m.shape[0]
    return pl.pallas_call(
        _residual_kernel,
        grid=(D_MODEL // TN, t // tm),
        in_specs=[pl.BlockSpec((tm, D_MODEL), lambda j, i: (i, 0)), pl.BlockSpec((D_MODEL, TN), lambda j, i: (0, j)),
                  pl.BlockSpec((tm, TN), lambda j, i: (i, j))],
        out_specs=pl.BlockSpec((tm, TN), lambda j, i: (i, j)),
        out_shape=jax.ShapeDtypeStruct((t, D_MODEL), F32),
        scratch_shapes=[pltpu.VMEM((D_MODEL, TN), BF16)],
        compiler_params=_cparams(("arbitrary", "arbitrary")),
        name="out_projection",
    )(m, w_out, x)


def _router_kernel(x_ref, g_ref, w_ref, b_ref, hl_ref, e_ref, wt_ref):
    x = x_ref[...]
    ms = jnp.mean(x * x, axis=-1, keepdims=True)
    h = (x * lax.rsqrt(ms + EPS)) * g_ref[...]
    _store_row_linear(hl_ref, h, 0, ROW_TILES)
    h_hi = h.astype(BF16)
    h_lo = (h - h_hi.astype(F32)).astype(BF16)
    both = jnp.dot(h_hi, w_ref[...], preferred_element_type=F32)
    logits = both[:, :LANES] + both[:, LANES:] + jnp.dot(h_lo, w_ref[:, :LANES], preferred_element_type=F32)
    logits = logits + b_ref[...]
    lane = lax.broadcasted_iota(jnp.int32, logits.shape, 1)
    neg = -jnp.inf
    big = jnp.int32(LANES)
    lg = jnp.where(lane < N_GROUPS, logits, neg)
    eg = jnp.exp(lg - jnp.max(lg, axis=-1, keepdims=True))
    pg = eg / jnp.sum(eg, axis=-1, keepdims=True)
    p_top = jnp.max(pg, axis=-1, keepdims=True)
    g_idx = jnp.min(jnp.where(pg == p_top, lane, big), axis=-1, keepdims=True)
    in_group = (lane >= N_GROUPS) & (lane < N_GROUPS + N_EXPERTS) & (((lane - N_GROUPS) // EXPERTS_PER_GROUP) == g_idx)
    le = jnp.where(in_group, logits, neg)
    ee = jnp.exp(le - jnp.max(le, axis=-1, keepdims=True))
    pe = jnp.where(in_group, ee / jnp.sum(ee, axis=-1, keepdims=True), -1.0)
    w1 = jnp.max(pe, axis=-1, keepdims=True)
    i1 = jnp.min(jnp.where(pe == w1, lane, big), axis=-1, keepdims=True)
    pe2 = jnp.where(lane == i1, -1.0, pe)
    w2 = jnp.max(pe2, axis=-1, keepdims=True)
    i2 = jnp.min(jnp.where(pe2 == w2, lane, big), axis=-1, keepdims=True)
    denom = w1 + w2
    e_ref[...] = jnp.where(lane == 0, i1 - N_GROUPS, jnp.where(lane == 1, i2 - N_GROUPS, 0))
    wt_ref[...] = jnp.where(lane == 0, w1 / denom * p_top, jnp.where(lane == 1, w2 / denom * p_top, 0.0))


def _router(x1, norm2_g, w_router, b_router, tm):
    t = x1.shape[0]
    return pl.pallas_call(
        _router_kernel,
        grid=(t // tm,),
        in_specs=[pl.BlockSpec((tm, D_MODEL), lambda i: (i, 0)), pl.BlockSpec((1, D_MODEL), lambda i: (0, 0)),
                  pl.BlockSpec((D_MODEL, 2 * LANES), lambda i: (0, 0)), pl.BlockSpec((1, LANES), lambda i: (0, 0))],
        out_specs=[pl.BlockSpec((tm * ROW_TILES, LANES), lambda i: (i, 0)),
                   pl.BlockSpec((tm, LANES), lambda i: (i, 0)), pl.BlockSpec((tm, LANES), lambda i: (i, 0))],
        out_shape=[jax.ShapeDtypeStruct((t * ROW_TILES, LANES), F32),
                   jax.ShapeDtypeStruct((t, LANES), jnp.int32), jax.ShapeDtypeStruct((t, LANES), F32)],
        compiler_params=_cparams(("arbitrary",)),
        name="router",
    )(x1, norm2_g.reshape(1, D_MODEL), w_router, b_router)


def _routing_plan(expert_idx, n_tiles):
    e_flat = expert_idx.reshape(-1)
    onehot = (e_flat[:, None] == jnp.arange(N_EXPERTS, dtype=jnp.int32)[None, :]).astype(jnp.int32)
    csum = jnp.cumsum(onehot, axis=0)
    counts = csum[-1]
    rank = jnp.sum(onehot * csum, axis=1) - 1
    tiles_e = (counts + MOE_TM - 1) // MOE_TM
    tile_end = jnp.cumsum(tiles_e)
    tile_start = tile_end - tiles_e
    dest_row = jnp.sum(onehot * tile_start[None, :], axis=1) * MOE_TM + rank
    n_used = tile_end[-1]
    tile_id = jnp.arange(n_tiles, dtype=jnp.int32)
    te = jnp.sum((tile_end[None, :] <= jnp.minimum(tile_id, n_used - 1)[:, None]).astype(jnp.int32), axis=1)
    te = jnp.minimum(te, N_EXPERTS - 1)
    left = counts[te] - (tile_id - tile_start[te]) * MOE_TM
    tile_rows = jnp.where(tile_id < n_used, jnp.clip(left, 0, MOE_TM), 0)
    i32 = lambda a: a.astype(jnp.int32)
    return i32(dest_row), i32(te), i32(tile_rows), i32(n_used).reshape(1)


def _dispatch_kernel(dr_ref, tr_ref, hp_hbm, hs_hbm, xs_hbm, zero_s, sem, zsem, *, n_prompt, n_sample, n_tiles):
    chunk_rows = DISPATCH_TOKENS * TOP_K * ROW_TILES

    def scatter(src_hbm, n_tok, pair0):
        def issue(c):
            for r in range(DISPATCH_TOKENS):
                tok = c * DISPATCH_TOKENS + r
                for k in range(TOP_K):
                    dst = dr_ref[pair0 + tok * TOP_K + k]
                    pltpu.make_async_copy(src_hbm.at[pl.ds(tok * ROW_TILES, ROW_TILES), :],
                                          xs_hbm.at[pl.ds(dst * ROW_TILES, ROW_TILES), :], sem).start()

        def wait_chunk():
            pltpu.make_async_copy(src_hbm.at[pl.ds(0, chunk_rows), :], xs_hbm.at[pl.ds(0, chunk_rows), :], sem).wait()

        n_chunks = n_tok // DISPATCH_TOKENS
        issue(0)

        def body(c, carry):
            issue(c)
            wait_chunk()
            return carry

        lax.fori_loop(1, n_chunks, body, 0)
        wait_chunk()

    scatter(hp_hbm, n_prompt, 0)
    scatter(hs_hbm, n_sample, n_prompt * TOP_K)

    zero_s[...] = jnp.zeros(zero_s.shape, F32)

    def pad_body(i, carry):
        n = tr_ref[i]

        @pl.when(n < MOE_TM)
        def _():
            rows = (MOE_TM - n) * ROW_TILES
            cp = pltpu.make_async_copy(zero_s.at[pl.ds(0, rows), :],
                                       xs_hbm.at[pl.ds((i * MOE_TM + n) * ROW_TILES, rows), :], zsem)
            cp.start()
            cp.wait()
        return carry

    lax.fori_loop(0, n_tiles, pad_body, 0)


def _dispatch(dest_row, tile_rows, h_lin_p, h_lin_s, n_tiles):
    n_prompt, n_sample = h_lin_p.shape[0] // ROW_TILES, h_lin_s.shape[0] // ROW_TILES
    any_spec = pl.BlockSpec(memory_space=pl.ANY)
    return pl.pallas_call(
        functools.partial(_dispatch_kernel, n_prompt=n_prompt, n_sample=n_sample, n_tiles=n_tiles),
        grid_spec=pltpu.PrefetchScalarGridSpec(
            num_scalar_prefetch=2,
            grid=(1,),
            in_specs=[any_spec, any_spec],
            out_specs=any_spec,
            scratch_shapes=[pltpu.VMEM((MOE_TM * ROW_TILES, LANES), F32), pltpu.SemaphoreType.DMA(()),
                            pltpu.SemaphoreType.DMA(())]),
        out_shape=jax.ShapeDtypeStruct((n_tiles * MOE_TM * ROW_TILES, LANES), F32),
        compiler_params=_cparams(("arbitrary",)),
        name="moe_dispatch",
    )(dest_row, tile_rows, h_lin_p, h_lin_s)


def _moe_kernel(te_ref, tr_ref, nu_ref, x_ref, wg_ref, wu_ref, wd_ref, y_ref, wg_s, wu_s, wd_s):
    i = pl.program_id(0)
    n = tr_ref[i]

    @pl.when(n == 0)
    def _():
        y_ref[...] = jnp.zeros(y_ref.shape, F32)

    @pl.when(n > 0)
    def _():
        changed = jnp.logical_or(i == 0, te_ref[i] != te_ref[jnp.maximum(i - 1, 0)])

        @pl.when(changed)
        def _():
            wg_s[...] = wg_ref[0].astype(BF16)
            wu_s[...] = wu_ref[0].astype(BF16)
            wd_s[...] = wd_ref[0].astype(BF16)

        x = _rows_to_tile(x_ref, 0, MOE_TM, ROW_TILES).astype(BF16)
        hg = jnp.dot(x, wg_s[...], preferred_element_type=F32)
        hu = jnp.dot(x, wu_s[...], preferred_element_type=F32)
        act = ((hg * jax.nn.sigmoid(hg)) * hu).astype(BF16)
        y = jnp.dot(act, wd_s[...], preferred_element_type=F32)
        _store_row_linear(y_ref, y, 0, ROW_TILES)


def _moe(x_sorted, tile_expert, tile_rows, n_used, w_gate, w_up, w_down, n_tiles):
    wmap = lambda i, te, tr, nu: (te[i], 0, 0)
    rows = MOE_TM * ROW_TILES
    return pl.pallas_call(
        _moe_kernel,
        grid_spec=pltpu.PrefetchScalarGridSpec(
            num_scalar_prefetch=3,
            grid=(n_tiles,),
            in_specs=[pl.BlockSpec((rows, LANES), lambda i, te, tr, nu: (jnp.minimum(i, nu[0] - 1), 0)),
                      pl.BlockSpec((1, D_MODEL, D_FF), wmap), pl.BlockSpec((1, D_MODEL, D_FF), wmap),
                      pl.BlockSpec((1, D_FF, D_MODEL), wmap)],
            out_specs=pl.BlockSpec((rows, LANES), lambda i, te, tr, nu: (i, 0)),
            scratch_shapes=[pltpu.VMEM((D_MODEL, D_FF), BF16), pltpu.VMEM((D_MODEL, D_FF), BF16),
                            pltpu.VMEM((D_FF, D_MODEL), BF16)]),
        out_shape=jax.ShapeDtypeStruct((n_tiles * rows, LANES), F32),
        compiler_params=_cparams(("arbitrary",)),
        name="moe_experts",
    )(tile_expert, tile_rows, n_used, x_sorted, w_gate, w_up, w_down)


def _combine_kernel(dr_ref, ys_hbm, x_ref, wt_ref, o_ref, g_s, sem, *, tm):
    i = pl.program_id(0)
    nt = pl.num_programs(0)
    slot = i % 2
    unroll = 8

    def issue(tile, dst_slot):
        def body(b, c):
            for u in range(unroll):
                r = b * unroll + u
                for k in range(TOP_K):
                    row = dr_ref[(tile * tm + r) * TOP_K + k]
                    pltpu.make_async_copy(ys_hbm.at[pl.ds(row * ROW_TILES, ROW_TILES), :],
                                          g_s.at[dst_slot, pl.ds((k * tm + r) * ROW_TILES, ROW_TILES), :],
                                          sem.at[dst_slot]).start()
            return c
        lax.fori_loop(0, tm // unroll, body, 0)

    @pl.when(i == 0)
    def _():
        issue(0, 0)

    @pl.when(i + 1 < nt)
    def _():
        issue(i + 1, 1 - slot)

    pltpu.make_async_copy(ys_hbm.at[pl.ds(0, TOP_K * tm * ROW_TILES), :], g_s.at[slot], sem.at[slot]).wait()
    wt = wt_ref[...]
    y0 = _rows_to_tile(g_s.at[slot], 0, tm, ROW_TILES)
    y1 = _rows_to_tile(g_s.at[slot], tm * ROW_TILES, tm, ROW_TILES)
    o_ref[...] = x_ref[...] + (wt[:, 0:1] * y0 + wt[:, 1:2] * y1)


def _combine(dest_row, y_sorted, x1, wt, tm):
    t = x1.shape[0]
    tile = lambda i, dr: (i, 0)
    return pl.pallas_call(
        functools.partial(_combine_kernel, tm=tm),
        grid_spec=pltpu.PrefetchScalarGridSpec(
            num_scalar_prefetch=1,
            grid=(t // tm,),
            in_specs=[pl.BlockSpec(memory_space=pl.ANY), pl.BlockSpec((tm, D_MODEL), tile),
                      pl.BlockSpec((tm, LANES), tile)],
            out_specs=pl.BlockSpec((tm, D_MODEL), tile),
            scratch_shapes=[pltpu.VMEM((2, TOP_K * tm * ROW_TILES, LANES), F32), pltpu.SemaphoreType.DMA((2,))]),
        out_shape=jax.ShapeDtypeStruct((t, D_MODEL), F32),
        compiler_params=_cparams(("arbitrary",)),
        name="moe_combine",
    )(dest_row, y_sorted, x1, wt)


def _rotary_tables(pos):
    inv_freq = ROPE_THETA ** (-jnp.arange(0, ROT_DIM, 2, dtype=F32) / ROT_DIM)
    ang = pos[:, None] * inv_freq[None, :]
    cos, sin = jnp.cos(ang), jnp.sin(ang)
    rest = LANES - ROT_DIM
    ones = jnp.ones((pos.shape[0], rest), F32)
    zeros = jnp.zeros((pos.shape[0], rest), F32)
    return jnp.concatenate([cos, cos, ones], axis=1), jnp.concatenate([-sin, sin, zeros], axis=1)


def _projections(x, pos, norm1_g, w_in, q_norm_g, k_norm_g):
    t = x.shape[0]
    tm = min(TM_MAX, t)
    h = _rmsnorm(x, norm1_g, tm)
    tcos, tsin = _rotary_tables(pos)
    c_q = 2 * D_CONV
    c_k = c_q + QK_WIDTH
    c_v = c_k + QK_WIDTH
    c_g = c_v + V_WIDTH
    rot_in = lambda g: [g.reshape(1, LANES), tcos, tsin]
    rot_specs = lambda order: [pl.BlockSpec((1, LANES), lambda a, b: (0, 0)),
                               pl.BlockSpec((tm, LANES), lambda a, b: ((a, b)[order], 0)),
                               pl.BlockSpec((tm, LANES), lambda a, b: ((a, b)[order], 0))]
    u_lin = _glu_call(h, w_in, tm)
    q_b = _stationary_call(_q_kernel, h, w_in, c_q, QK_WIDTH, rot_in(q_norm_g), rot_specs(1), BF16, tm, "proj_q")
    k_lin, k_b = _row_linear_call(_k_kernel, h, w_in, c_k, QK_WIDTH, rot_in(k_norm_g), rot_specs(0), tm, "proj_k")
    v_lin, v_b = _row_linear_call(_v_kernel, h, w_in, c_v, V_WIDTH, [], [], tm, "proj_v")
    gates = _stationary_call(_gate_kernel, h, w_in, c_g, 2 * D_MODEL, [], [], F32, tm, "proj_gates")
    return u_lin, q_b, k_lin, k_b, v_lin, v_b, gates


def _layer(x_p, x_s, cache_k, cache_v, state_conv, page_table, lambda_init,
           norm1_g, w_in, conv_w, conv_b, conv_ln_g, conv_ln_b, w_conv_out, q_norm_g, k_norm_g,
           lambda_q1, lambda_k1, lambda_q2, lambda_k2, subln_g, w_attn_out, w_out, norm2_g,
           w_router_group, b_router_group, w_router_expert, b_router_expert, w_gate_e, w_up_e, w_down_e):
    batch, seq, _ = x_p.shape
    nb, n_new, _ = x_s.shape
    past = page_table.shape[1] * PAGE
    tp, ts = batch * seq, nb * n_new
    xp, xs = x_p.reshape(tp, D_MODEL), x_s.reshape(ts, D_MODEL)
    pos_p = jnp.tile(jnp.arange(seq, dtype=F32), batch)
    pos_s = jnp.tile(past + jnp.arange(n_new, dtype=F32), nb)
    u_p, q_p, kl_p, kb_p, vl_p, vb_p, gates_p = _projections(xp, pos_p, norm1_g, w_in, q_norm_g, k_norm_g)
    u_s, q_s, kl_s, kb_s, vl_s, vb_s, gates_s = _projections(xs, pos_s, norm1_g, w_in, q_norm_g, k_norm_g)

    u_p = u_p.reshape(batch, seq, CONV_TILES, LANES)
    u_s = u_s.reshape(nb, n_new, CONV_TILES, LANES)
    hist_p = jnp.concatenate([jnp.zeros((batch, HIST, CONV_TILES, LANES), F32), u_p], axis=1)
    hist_s = jnp.concatenate([state_conv.reshape(nb, HIST, CONV_TILES, LANES), u_s], axis=1)
    conv = lambda hist, name: _conv_branch(hist, conv_w, conv_b, conv_ln_g, conv_ln_b, name)
    yc_p = conv(hist_p, "conv_prompt").reshape(tp * CONV_TILES, LANES)
    yc_s = conv(hist_s, "conv_sample").reshape(ts * CONV_TILES, LANES)

    lam_rows = jnp.stack([lambda_q1, lambda_k1, lambda_q2, lambda_k2]).astype(F32)
    o_p = _prompt_attention(q_p, kb_p, vb_p, lam_rows, subln_g, batch, seq, lambda_init)
    pad_rows = lambda a: jnp.pad(a.reshape(nb, n_new, -1), ((0, 0), (0, NEW_ROWS - n_new), (0, 0)))
    o_s = _sample_attention(_query_rows(q_s.reshape(nb, n_new, QK_WIDTH)), pad_rows(kb_s), pad_rows(vb_s),
                            cache_k, cache_v, page_table, lam_rows, subln_g, lambda_init, n_new)
    o_s = o_s[:, :n_new].reshape(ts, V_WIDTH).astype(BF16)

    w_r = jnp.zeros((D_MODEL, LANES), F32).at[:, :N_GROUPS].set(w_router_group)
    w_r = w_r.at[:, N_GROUPS:N_GROUPS + N_EXPERTS].set(w_router_expert)
    w_hi = w_r.astype(BF16)
    w_lo = (w_r - w_hi.astype(F32)).astype(BF16)
    w_split = jnp.concatenate([w_hi, w_lo], axis=1)
    b_r = jnp.zeros((1, LANES), F32).at[0, :N_GROUPS].set(b_router_group)
    b_r = b_r.at[0, N_GROUPS:N_GROUPS + N_EXPERTS].set(b_router_expert)

    def tail(x, yc, o_n, gates):
        tm = min(TM_MAX, x.shape[0])
        m = _merge(yc, o_n, w_conv_out, w_attn_out, gates, tm)
        x1 = _out_projection(m, w_out, x, tm)
        return (x1,) + tuple(_router(x1, norm2_g, w_split, b_r, min(512, x.shape[0])))

    x1_p, hl_p, e_p, wt_p = tail(xp, yc_p, o_p, gates_p)
    x1_s, hl_s, e_s, wt_s = tail(xs, yc_s, o_s, gates_s)

    n_pairs = (tp + ts) * TOP_K
    n_tiles = -(-n_pairs // MOE_TM) + N_EXPERTS
    e_all = jnp.concatenate([e_p[:, :TOP_K], e_s[:, :TOP_K]], axis=0)
    dest_row, te, tile_rows, n_used = _routing_plan(e_all, n_tiles)
    x_sorted = _dispatch(dest_row, tile_rows, hl_p, hl_s, n_tiles)
    y_sorted = _moe(x_sorted, te, tile_rows, n_used, w_gate_e, w_up_e, w_down_e, n_tiles)
    y_p = _combine(dest_row[:tp * TOP_K], y_sorted, x1_p, wt_p, CMB_TM)
    y_s = _combine(dest_row[tp * TOP_K:], y_sorted, x1_s, wt_s, min(CMB_TM, ts))

    v_out = lambda a, lead: a.reshape(lead + (2, N_HEADS, LANES)).swapaxes(-3, -2).reshape(lead + (N_HEADS, V_DIM))
    return (y_p.reshape(batch, seq, D_MODEL), y_s.reshape(nb, n_new, D_MODEL),
            kl_p.reshape(batch, seq, N_HEADS, 2, HEAD_DIM), v_out(vl_p, (batch, seq)),
            hist_p[:, seq:].reshape(batch, HIST, D_CONV),
            kl_s.reshape(nb, n_new, N_HEADS, 2, HEAD_DIM), v_out(vl_s, (nb, n_new)),
            hist_s[:, n_new:].reshape(nb, HIST, D_CONV))


def kernel(x_prompt, x_sample, cache_k, cache_v, state_conv, page_table, norm1_g, w_in, conv_w, conv_b, conv_ln_g, conv_ln_b, w_conv_out, q_norm_g, k_norm_g, lambda_q1, lambda_k1, lambda_q2, lambda_k2, subln_g, w_attn_out, w_out, norm2_g, w_router_group, b_router_group, w_router_expert, b_router_expert, w_gate_e, w_up_e, w_down_e):
    depth = norm1_g.shape[0]
    x_p, x_s = x_prompt, x_sample
    outs = [[] for _ in range(6)]
    for l in range(depth):
        lambda_init = 0.8 - 0.6 * math.exp(-0.3 * l)
        x_p, x_s, *rest = _layer(
            x_p, x_s, cache_k[l], cache_v[l], state_conv[l], page_table, lambda_init,
            norm1_g[l], w_in[l], conv_w[l], conv_b[l], conv_ln_g[l], conv_ln_b[l], w_conv_out[l],
            q_norm_g[l], k_norm_g[l], lambda_q1[l], lambda_k1[l], lambda_q2[l], lambda_k2[l], subln_g[l],
            w_attn_out[l], w_out[l], norm2_g[l], w_router_group[l], b_router_group[l],
            w_router_expert[l], b_router_expert[l], w_gate_e[l], w_up_e[l], w_down_e[l])
        for acc, val in zip(outs, rest):
            acc.append(val)
    return (x_p, x_s) + tuple(jnp.stack(o) for o in outs)
```

```python
import functools
import math

import jax
import jax.numpy as jnp
from jax import lax
from jax.experimental import pallas as pl
from jax.experimental.pallas import tpu as pltpu

F32 = jnp.float32
BF16 = jnp.bfloat16

D_MODEL = 2048
N_HEADS = 8
HEAD_DIM = 128
V_DIM = 2 * HEAD_DIM
QK_WIDTH = N_HEADS * 2 * HEAD_DIM
V_WIDTH = N_HEADS * V_DIM
ROT_DIM = HEAD_DIM // 4
ROPE_THETA = 500000.0
ATTN_SCALE = HEAD_DIM ** -0.5
LOG2E = math.log2(math.e)
NEG_INF = -1e30
D_CONV = D_MODEL // 2
CONV_WIDTH = 31
HIST = CONV_WIDTH - 1
N_GROUPS = 4
EXPERTS_PER_GROUP = 8
N_EXPERTS = N_GROUPS * EXPERTS_PER_GROUP
TOP_K = 2
D_FF = D_MODEL // 4
EPS = 1e-6
PAGE = 128

LANES = 128
SUBLANES = 8
ROW_TILES = D_MODEL // LANES
CONV_TILES = D_CONV // LANES
VMEM_LIMIT = 56 * 1024 * 1024

TM_MAX = 1024
TN = 512
ATTN_ROWS = 256
PAGES_PER_STEP = 8
QROWS = 2 * 4 * SUBLANES
NEW_ROWS = 16
MOE_TM = 256
CMB_TM = 256
CONV_ROWS = 32


def _cparams(sem):
    return pltpu.CompilerParams(dimension_semantics=sem, vmem_limit_bytes=VMEM_LIMIT)


def _rows_to_tile(buf, base, rows, tiles):
    return jnp.concatenate([buf[pl.ds(base + c, rows, stride=tiles), :] for c in range(tiles)], axis=1)


def _rmsnorm_kernel(x_ref, g_ref, o_ref):
    x = x_ref[...]
    ms = jnp.mean(x * x, axis=-1, keepdims=True)
    o_ref[...] = ((x * lax.rsqrt(ms + EPS)) * g_ref[...]).astype(o_ref.dtype)


def _rmsnorm(x, g, tm):
    t, d = x.shape
    return pl.pallas_call(
        _rmsnorm_kernel,
        grid=(t // tm,),
        in_specs=[pl.BlockSpec((tm, d), lambda i: (i, 0)), pl.BlockSpec((1, d), lambda i: (0, 0))],
        out_specs=pl.BlockSpec((tm, d), lambda i: (i, 0)),
        out_shape=jax.ShapeDtypeStruct((t, d), BF16),
        compiler_params=_cparams(("arbitrary",)),
        name="rmsnorm1",
    )(x, g.reshape(1, d))


def _cache_weight(w_ref, w_s):
    @pl.when(pl.program_id(1) == 0)
    def _():
        w_s[...] = w_ref[...].astype(BF16)


def _store_row_linear(o_ref, val, first_tile, tiles, row_of_tile=None):
    rows = val.shape[0]
    for c in range(val.shape[1] // LANES):
        tile = first_tile + c
        dst = tile if row_of_tile is None else row_of_tile(tile)
        o_ref[pl.ds(dst, rows, stride=tiles), :] = val[:, c * LANES:(c + 1) * LANES]


def _glu_kernel(h_ref, wa_ref, wb_ref, u_ref):
    h = h_ref[...]
    a = jnp.dot(h, wa_ref[...].astype(BF16), preferred_element_type=F32)
    b = jnp.dot(h, wb_ref[...].astype(BF16), preferred_element_type=F32)
    _store_row_linear(u_ref, a * jax.nn.sigmoid(b), pl.program_id(1) * (TN // LANES), CONV_TILES)


def _glu_call(h, w, tm):
    t, k = h.shape
    nj = D_CONV // TN
    return pl.pallas_call(
        _glu_kernel,
        grid=(t // tm, nj),
        in_specs=[pl.BlockSpec((tm, k), lambda i, j: (i, 0)), pl.BlockSpec((k, TN), lambda i, j: (0, j)),
                  pl.BlockSpec((k, TN), lambda i, j: (0, nj + j))],
        out_specs=pl.BlockSpec((tm * CONV_TILES, LANES), lambda i, j: (i, 0)),
        out_shape=jax.ShapeDtypeStruct((t * CONV_TILES, LANES), F32),
        compiler_params=_cparams(("arbitrary", "arbitrary")),
        name="proj_glu",
    )(h, w, w)


def _head_norm_rotary(acc, ones_ref, gain, tcos, tsin):
    ssq = jnp.dot((acc * acc).astype(BF16), ones_ref[...], preferred_element_type=F32)
    tn = acc * lax.rsqrt(ssq * (1.0 / HEAD_DIM) + EPS)
    lane = lax.broadcasted_iota(jnp.int32, (acc.shape[0], LANES), 1)
    half = ROT_DIM // 2
    outs = []
    for g in range(acc.shape[1] // LANES):
        t = tn[:, g * LANES:(g + 1) * LANES] * gain
        partner = jnp.where(lane < half, pltpu.roll(t, LANES - half, 1), pltpu.roll(t, half, 1))
        outs.append(t * tcos + partner * tsin)
    return jnp.concatenate(outs, axis=1)


def _q_kernel(h_ref, w_ref, g_ref, tc_ref, ts_ref, ones_ref, ob_ref, w_s):
    _cache_weight(w_ref, w_s)
    acc = jnp.dot(h_ref[...], w_s[...], preferred_element_type=F32)
    y = _head_norm_rotary(acc, ones_ref, g_ref[...], tc_ref[...], ts_ref[...])
    ob_ref[...] = (y * (ATTN_SCALE * LOG2E)).astype(BF16)


def _gate_kernel(h_ref, w_ref, o_ref, w_s):
    _cache_weight(w_ref, w_s)
    o_ref[...] = jax.nn.sigmoid(jnp.dot(h_ref[...], w_s[...], preferred_element_type=F32))


def _stationary_call(kernel, h, w, col0, width, extra_in, extra_specs, out_dtype, tm, name):
    t, k = h.shape
    return pl.pallas_call(
        kernel,
        grid=(width // TN, t // tm),
        in_specs=[pl.BlockSpec((tm, k), lambda j, i: (i, 0)),
                  pl.BlockSpec((k, TN), lambda j, i: (0, col0 // TN + j))] + extra_specs,
        out_specs=pl.BlockSpec((tm, TN), lambda j, i: (i, j)),
        out_shape=jax.ShapeDtypeStruct((t, width), out_dtype),
        scratch_shapes=[pltpu.VMEM((k, TN), BF16)],
        compiler_params=_cparams(("arbitrary", "arbitrary")),
        name=name,
    )(h, w, *extra_in)


def _k_kernel(h_ref, w_ref, g_ref, tc_ref, ts_ref, ones_ref, of_ref, ob_ref):
    acc = jnp.dot(h_ref[...], w_ref[...].astype(BF16), preferred_element_type=F32)
    y = _head_norm_rotary(acc, ones_ref, g_ref[...], tc_ref[...], ts_ref[...])
    _store_row_linear(of_ref, y, pl.program_id(1) * (TN // LANES), ROW_TILES)
    ob_ref[...] = y.astype(BF16)


def _v_kernel(h_ref, w_ref, of_ref, ob_ref):
    acc = jnp.dot(h_ref[...], w_ref[...].astype(BF16), preferred_element_type=F32)
    _store_row_linear(of_ref, acc, pl.program_id(1) * (TN // LANES), ROW_TILES,
                      row_of_tile=lambda tile: (tile % 2) * N_HEADS + tile // 2)
    ob_ref[...] = acc.astype(BF16)


def _row_linear_call(kernel, h, w, col0, width, extra_in, extra_specs, tm, name):
    t, k = h.shape
    tiles = width // LANES
    return pl.pallas_call(
        kernel,
        grid=(t // tm, width // TN),
        in_specs=[pl.BlockSpec((tm, k), lambda i, j: (i, 0)),
                  pl.BlockSpec((k, TN), lambda i, j: (0, col0 // TN + j))] + extra_specs,
        out_specs=[pl.BlockSpec((tm * tiles, LANES), lambda i, j: (i, 0)), pl.BlockSpec((tm, TN), lambda i, j: (i, j))],
        out_shape=[jax.ShapeDtypeStruct((t * tiles, LANES), F32), jax.ShapeDtypeStruct((t, width), BF16)],
        compiler_params=_cparams(("arbitrary", "arbitrary")),
        name=name,
    )(h, w, *extra_in)


def _conv_kernel(hist_ref, w_ref, b_ref, g_ref, beta_ref, o_ref, *, length):
    rows = min(CONV_ROWS, length)

    def step(i, carry):
        t0 = i * rows
        acc = jnp.zeros((rows, CONV_TILES, LANES), F32)
        for j in range(CONV_WIDTH):
            acc = acc + hist_ref[0, pl.ds(t0 + j, rows)] * w_ref[j]
        y = acc + b_ref[...]
        mu = jnp.mean(y, axis=(1, 2), keepdims=True)
        yc = y - mu
        var = jnp.mean(yc * yc, axis=(1, 2), keepdims=True)
        z = yc * lax.rsqrt(var + EPS) * g_ref[...] + beta_ref[...]
        o_ref[0, pl.ds(t0, rows)] = z * jax.nn.sigmoid(z)
        return carry

    trips = length // rows
    lax.fori_loop(0, trips, step, 0, unroll=2 if trips % 2 == 0 else 1)


def _conv_branch(hist, conv_w, conv_b, ln_g, ln_b, name):
    b, lh = hist.shape[:2]
    length = lh - HIST
    tile = lambda a: a.reshape(-1, CONV_TILES, LANES)
    whole = lambda n: pl.BlockSpec((n, CONV_TILES, LANES), lambda i: (0, 0, 0))
    return pl.pallas_call(
        functools.partial(_conv_kernel, length=length),
        grid=(b,),
        in_specs=[pl.BlockSpec((1, lh, CONV_TILES, LANES), lambda i: (i, 0, 0, 0)),
                  whole(CONV_WIDTH), whole(1), whole(1), whole(1)],
        out_specs=pl.BlockSpec((1, length, CONV_TILES, LANES), lambda i: (i, 0, 0, 0)),
        out_shape=jax.ShapeDtypeStruct((b, length, CONV_TILES, LANES), F32),
        compiler_params=_cparams(("arbitrary",)),
        name=name,
    )(hist, tile(conv_w), tile(conv_b), tile(ln_g), tile(ln_b))


def _lambda_value(lam_ref, lambda_init):
    v = lam_ref[...]
    s1 = jnp.sum(v[0:1] * v[1:2], axis=-1, keepdims=True)
    s2 = jnp.sum(v[2:3] * v[3:4], axis=-1, keepdims=True)
    return jnp.exp(s1) - jnp.exp(s2) + lambda_init


def _prompt_attn_kernel(q_ref, k_ref, v_ref, lam_ref, sg_ref, o_ref, *, lambda_init, seq):
    rb = ATTN_ROWS
    lam = _lambda_value(lam_ref, lambda_init)
    row = lax.broadcasted_iota(jnp.int32, (rb, rb), 0)
    col = lax.broadcasted_iota(jnp.int32, (rb, rb), 1)
    for r in range(seq // rb):
        rows = pl.ds(r * rb, rb)
        kc = (r + 1) * rb
        v = v_ref[0:kc, :]
        outs = []
        for c in range(2):
            q = q_ref[rows, c * HEAD_DIM:(c + 1) * HEAD_DIM]
            k = k_ref[0:kc, c * HEAD_DIM:(c + 1) * HEAD_DIM]
            s = lax.dot_general(q, k, (((1,), (1,)), ((), ())), preferred_element_type=F32)
            diag = jnp.where(col <= row, s[:, kc - rb:], NEG_INF)
            s = diag if r == 0 else jnp.concatenate([s[:, :kc - rb], diag], axis=1)
            p = jnp.exp2(s - jnp.max(s, axis=-1, keepdims=True))
            l = jnp.sum(p, axis=-1, keepdims=True)
            outs.append(jnp.dot(p.astype(BF16), v, preferred_element_type=F32) / l)
        o = outs[0] - lam * outs[1]
        ms = jnp.mean(o * o, axis=-1, keepdims=True)
        o_ref[rows, :] = (((o * lax.rsqrt(ms + EPS)) * sg_ref[...]) * (1.0 - lambda_init)).astype(o_ref.dtype)


def _prompt_attention(q, k, v, lam_rows, subln_g, batch, seq, lambda_init):
    blk = lambda b, h: (b, h)
    return pl.pallas_call(
        functools.partial(_prompt_attn_kernel, lambda_init=lambda_init, seq=seq),
        grid=(batch, N_HEADS),
        in_specs=[pl.BlockSpec((seq, V_DIM), blk), pl.BlockSpec((seq, V_DIM), blk), pl.BlockSpec((seq, V_DIM), blk),
                  pl.BlockSpec((4, HEAD_DIM), lambda b, h: (0, 0)), pl.BlockSpec((1, V_DIM), lambda b, h: (0, 0))],
        out_specs=pl.BlockSpec((seq, V_DIM), blk),
        out_shape=jax.ShapeDtypeStruct((batch * seq, V_WIDTH), BF16),
        compiler_params=_cparams(("arbitrary", "arbitrary")),
        name="prompt_attention",
    )(q, k, v, lam_rows, subln_g.reshape(1, V_DIM))


def _page_tiles(x):
    tiles = jnp.swapaxes(x, 0, 1)
    return [tiles[i].astype(BF16) for i in range(SUBLANES)]


def _sample_attn_kernel(pt_ref, *refs, n_new, lambda_init):
    g = PAGES_PER_STEP
    k_refs, v_refs = refs[:g], refs[g:2 * g]
    q_ref, kn_ref, vn_ref, lam_ref, sg_ref, o_ref, m_s, l_s, acc_s = refs[2 * g:]
    j = pl.program_id(1)
    q2 = q_ref[0]
    nt_dims = (((1,), (1,)), ((), ()))

    @pl.when(j == 0)
    def _():
        m_s[...] = jnp.full(m_s.shape, NEG_INF, F32)
        l_s[...] = jnp.zeros(l_s.shape, F32)
        acc_s[...] = jnp.zeros(acc_s.shape, F32)

    def online_update(s, values):
        m_old = m_s[...]
        m_new = jnp.maximum(m_old, jnp.max(s, axis=-1, keepdims=True))
        alpha = jnp.exp2(m_old - m_new)
        p = jnp.exp2(s - m_new)
        l_s[...] = alpha * l_s[...] + jnp.sum(p, axis=-1, keepdims=True)
        pv = None
        c0 = 0
        for n, v in values:
            part = jnp.dot(p[:, c0:c0 + n].astype(BF16), v, preferred_element_type=F32)
            pv = part if pv is None else pv + part
            c0 += n
        acc_s[...] = alpha * acc_s[...] + pv
        m_s[...] = m_new

    s_parts, values = [], []
    for p in range(g):
        kfull = jnp.concatenate(_page_tiles(k_refs[p][:, 0]) + _page_tiles(k_refs[p][:, 1]), axis=1)
        s_parts.append(lax.dot_general(q2, kfull, nt_dims, preferred_element_type=F32))
        t0, t1 = _page_tiles(v_refs[p][:, :, :LANES]), _page_tiles(v_refs[p][:, :, LANES:])
        values.append((PAGE, jnp.concatenate([t for pair in zip(t0, t1) for t in pair], axis=1)))
    online_update(jnp.concatenate(s_parts, axis=1), values)

    @pl.when(j == pl.num_programs(1) - 1)
    def _():
        kn = kn_ref[0]
        s = lax.dot_general(q2, kn, nt_dims, preferred_element_type=F32)
        key = lax.broadcasted_iota(jnp.int32, s.shape, 1)
        qry = (lax.broadcasted_iota(jnp.int32, s.shape, 0) % (4 * SUBLANES)) // SUBLANES
        s = jnp.where((key <= qry) & (key < n_new), s, NEG_INF)
        online_update(s, [(kn.shape[0], vn_ref[0])])
        lam = _lambda_value(lam_ref, lambda_init)
        half = QROWS // 2
        acc, l = acc_s[...], l_s[...]
        o_all = acc[:half] / l[:half] - lam * (acc[half:] / l[half:])
        head = lax.broadcasted_iota(jnp.int32, (SUBLANES, V_WIDTH), 0)
        own = head == lax.broadcasted_iota(jnp.int32, (SUBLANES, V_WIDTH), 1) // V_DIM
        rows = []
        for q in range(n_new):
            o = jnp.where(own, o_all[q * SUBLANES:(q + 1) * SUBLANES, :], 0.0)
            ms = jnp.sum(o * o, axis=-1, keepdims=True) / V_DIM
            o = jnp.sum(o * lax.rsqrt(ms + EPS), axis=0, keepdims=True)
            rows.append((o * sg_ref[...]) * (1.0 - lambda_init))
        rows.append(jnp.zeros((SUBLANES - n_new, V_WIDTH), F32))
        o_ref[0] = jnp.concatenate(rows, axis=0)


def _sample_attention(q_rows, k_new, v_new, cache_k, cache_v, page_table, lam_rows, subln_g, lambda_init, n_new):
    nb, n_pages = page_table.shape
    g = PAGES_PER_STEP
    n_pool = cache_k.shape[0]
    ck = cache_k.reshape(n_pool * PAGE, 2, SUBLANES, HEAD_DIM)
    cv = cache_v.reshape(n_pool * PAGE, N_HEADS, V_DIM)
    pt = page_table.reshape(-1)
    kmap = lambda p: (lambda b, j, pt: (pt[b * n_pages + j * g + p], 0, 0, 0))
    vmap = lambda p: (lambda b, j, pt: (pt[b * n_pages + j * g + p], 0, 0))
    k_specs = [pl.BlockSpec((PAGE, 2, SUBLANES, HEAD_DIM), kmap(p)) for p in range(g)]
    v_specs = [pl.BlockSpec((PAGE, N_HEADS, V_DIM), vmap(p)) for p in range(g)]
    new_rows = k_new.shape[1]
    per_b = lambda shape: pl.BlockSpec((1,) + shape, lambda b, j, pt: (b, 0, 0))
    const = lambda shape: pl.BlockSpec(shape, lambda b, j, pt: (0, 0))
    tiled_g = jnp.tile(subln_g.reshape(1, V_DIM), (1, N_HEADS))
    return pl.pallas_call(
        functools.partial(_sample_attn_kernel, n_new=n_new, lambda_init=lambda_init),
        grid_spec=pltpu.PrefetchScalarGridSpec(
            num_scalar_prefetch=1,
            grid=(nb, n_pages // g),
            in_specs=k_specs + v_specs + [per_b((QROWS, QK_WIDTH)), per_b((new_rows, QK_WIDTH)),
                                          per_b((new_rows, V_WIDTH)), const((4, HEAD_DIM)), const((1, V_WIDTH))],
            out_specs=per_b((SUBLANES, V_WIDTH)),
            scratch_shapes=[pltpu.VMEM((QROWS, 1), F32), pltpu.VMEM((QROWS, 1), F32), pltpu.VMEM((QROWS, V_WIDTH), F32)]),
        out_shape=jax.ShapeDtypeStruct((nb, SUBLANES, V_WIDTH), F32),
        compiler_params=_cparams(("arbitrary", "arbitrary")),
        name="sample_attention",
    )(pt, *([ck] * g), *([cv] * g), q_rows, k_new, v_new, lam_rows, tiled_g)


def _query_rows(q_s):
    n_new = q_s.shape[1]
    r = jnp.arange(QROWS)
    r_c, r_q, r_h = r // (4 * SUBLANES), (r % (4 * SUBLANES)) // SUBLANES, r % SUBLANES
    col_hc = jnp.arange(QK_WIDTH) // HEAD_DIM
    keep = (col_hc[None, :] == (r_h * 2 + r_c)[:, None]) & (r_q < n_new)[:, None]
    picked = q_s[:, jnp.minimum(r_q, n_new - 1), :]
    return jnp.where(keep[None], picked, jnp.zeros((), q_s.dtype))


def _merge_kernel(yc_ref, ya_ref, wc_ref, wa_ref, gc_ref, ga_ref, m_ref, wc_s, wa_s):
    _cache_weight(wc_ref, wc_s)
    _cache_weight(wa_ref, wa_s)
    conv = _rows_to_tile(yc_ref, 0, ya_ref.shape[0], CONV_TILES).astype(BF16)
    yc = jnp.dot(conv, wc_s[...], preferred_element_type=F32)
    ya = jnp.dot(ya_ref[...], wa_s[...], preferred_element_type=F32)
    m_ref[...] = (gc_ref[...] * yc + ga_ref[...] * ya).astype(m_ref.dtype)


def _merge(yconv_lin, o_n, w_conv_out, w_attn_out, gates, tm):
    t = o_n.shape[0]
    nj, ni = D_MODEL // TN, t // tm
    return pl.pallas_call(
        _merge_kernel,
        grid=(nj, ni),
        in_specs=[pl.BlockSpec((tm * CONV_TILES, LANES), lambda j, i: (i, 0)),
                  pl.BlockSpec((tm, V_WIDTH), lambda j, i: (i, 0)),
                  pl.BlockSpec((D_CONV, TN), lambda j, i: (0, j)), pl.BlockSpec((V_WIDTH, TN), lambda j, i: (0, j)),
                  pl.BlockSpec((tm, TN), lambda j, i: (i, j)), pl.BlockSpec((tm, TN), lambda j, i: (i, D_MODEL // TN + j))],
        out_specs=pl.BlockSpec((tm, TN), lambda j, i: (i, j)),
        out_shape=jax.ShapeDtypeStruct((t, D_MODEL), BF16),
        scratch_shapes=[pltpu.VMEM((D_CONV, TN), BF16), pltpu.VMEM((V_WIDTH, TN), BF16)],
        compiler_params=_cparams(("arbitrary", "arbitrary")),
        name="merge",
    )(yconv_lin, o_n, w_conv_out, w_attn_out, gates, gates)


def _residual_kernel(m_ref, w_ref, x_ref, o_ref, w_s):
    _cache_weight(w_ref, w_s)
    o_ref[...] = x_ref[...] + jnp.dot(m_ref[...], w_s[...], preferred_element_type=F32)


def _out_projection(m, w_out, x, tm):
    t = m.shape[0]
    return pl.pallas_call(
        _residual_kernel,
        grid=(D_MODEL // TN, t // tm),
        in_specs=[pl.BlockSpec((tm, D_MODEL), lambda j, i: (i, 0)), pl.BlockSpec((D_MODEL, TN), lambda j, i: (0, j)),
                  pl.BlockSpec((tm, TN), lambda j, i: (i, j))],
        out_specs=pl.BlockSpec((tm, TN), lambda j, i: (i, j)),
        out_shape=jax.ShapeDtypeStruct((t, D_MODEL), F32),
        scratch_shapes=[pltpu.VMEM((D_MODEL, TN), BF16)],
        compiler_params=_cparams(("arbitrary", "arbitrary")),
        name="out_projection",
    )(m, w_out, x)


def _router_kernel(x_ref, g_ref, w_ref, b_ref, hl_ref, e_ref, wt_ref):
    x = x_ref[...]
    ms = jnp.mean(x * x, axis=-1, keepdims=True)
    h = (x * lax.rsqrt(ms + EPS)) * g_ref[...]
    _store_row_linear(hl_ref, h, 0, ROW_TILES)
    h_hi = h.astype(BF16)
    h_lo = (h - h_hi.astype(F32)).astype(BF16)
    both = jnp.dot(h_hi, w_ref[...], preferred_element_type=F32)
    logits = both[:, :LANES] + both[:, LANES:] + jnp.dot(h_lo, w_ref[:, :LANES], preferred_element_type=F32)
    logits = logits + b_ref[...]
    lane = lax.broadcasted_iota(jnp.int32, logits.shape, 1)
    neg = -jnp.inf
    big = jnp.int32(LANES)
    lg = jnp.where(lane < N_GROUPS, logits, neg)
    eg = jnp.exp(lg - jnp.max(lg, axis=-1, keepdims=True))
    pg = eg / jnp.sum(eg, axis=-1, keepdims=True)
    p_top = jnp.max(pg, axis=-1, keepdims=True)
    g_idx = jnp.min(jnp.where(pg == p_top, lane, big), axis=-1, keepdims=True)
    in_group = (lane >= N_GROUPS) & (lane < N_GROUPS + N_EXPERTS) & (((lane - N_GROUPS) // EXPERTS_PER_GROUP) == g_idx)
    le = jnp.where(in_group, logits, neg)
    ee = jnp.exp(le - jnp.max(le, axis=-1, keepdims=True))
    pe = jnp.where(in_group, ee / jnp.sum(ee, axis=-1, keepdims=True), -1.0)
    w1 = jnp.max(pe, axis=-1, keepdims=True)
    i1 = jnp.min(jnp.where(pe == w1, lane, big), axis=-1, keepdims=True)
    pe2 = jnp.where(lane == i1, -1.0, pe)
    w2 = jnp.max(pe2, axis=-1, keepdims=True)
    i2 = jnp.min(jnp.where(pe2 == w2, lane, big), axis=-1, keepdims=True)
    denom = w1 + w2
    e_ref[...] = jnp.where(lane == 0, i1 - N_GROUPS, jnp.where(lane == 1, i2 - N_GROUPS, 0))
    wt_ref[...] = jnp.where(lane == 0, w1 / denom * p_top, jnp.where(lane == 1, w2 / denom * p_top, 0.0))


def _router(x1, norm2_g, w_router, b_router, tm):
    t = x1.shape[0]
    return pl.pallas_call(
        _router_kernel,
        grid=(t // tm,),
        in_specs=[pl.BlockSpec((tm, D_MODEL), lambda i: (i, 0)), pl.BlockSpec((1, D_MODEL), lambda i: (0, 0)),
                  pl.BlockSpec((D_MODEL, 2 * LANES), lambda i: (0, 0)), pl.BlockSpec((1, LANES), lambda i: (0, 0))],
        out_specs=[pl.BlockSpec((tm * ROW_TILES, LANES), lambda i: (i, 0)),
                   pl.BlockSpec((tm, LANES), lambda i: (i, 0)), pl.BlockSpec((tm, LANES), lambda i: (i, 0))],
        out_shape=[jax.ShapeDtypeStruct((t * ROW_TILES, LANES), F32),
                   jax.ShapeDtypeStruct((t, LANES), jnp.int32), jax.ShapeDtypeStruct((t, LANES), F32)],
        compiler_params=_cparams(("arbitrary",)),
        name="router",
    )(x1, norm2_g.reshape(1, D_MODEL), w_router, b_router)


def _routing_plan(expert_idx, n_tiles):
    e_flat = expert_idx.reshape(-1)
    onehot = (e_flat[:, None] == jnp.arange(N_EXPERTS, dtype=jnp.int32)[None, :]).astype(jnp.int32)
    csum = jnp.cumsum(onehot, axis=0)
    counts = csum[-1]
    rank = jnp.sum(onehot * csum, axis=1) - 1
    tiles_e = (counts + MOE_TM - 1) // MOE_TM
    tile_end = jnp.cumsum(tiles_e)
    tile_start = tile_end - tiles_e
    dest_row = jnp.sum(onehot * tile_start[None, :], axis=1) * MOE_TM + rank
    n_used = tile_end[-1]
    tile_id = jnp.arange(n_tiles, dtype=jnp.int32)
    te = jnp.sum((tile_end[None, :] <= jnp.minimum(tile_id, n_used - 1)[:, None]).astype(jnp.int32), axis=1)
    te = jnp.minimum(te, N_EXPERTS - 1)
    left = counts[te] - (tile_id - tile_start[te]) * MOE_TM
    tile_rows = jnp.where(tile_id < n_used, jnp.clip(left, 0, MOE_TM), 0)
    i32 = lambda a: a.astype(jnp.int32)
    return i32(dest_row), i32(te), i32(tile_rows), i32(n_used).reshape(1)


def _dispatch_kernel(dr_ref, tr_ref, hp_ref, hs_ref, xs_hbm, zero_s, sem, zsem, *, tile_p, p_steps, n_sample, n_tiles):
    i = pl.program_id(0)
    unroll = 8

    def scatter(src_ref, n_tok, pair0):
        def body(b, c):
            for u in range(unroll):
                r = b * unroll + u
                for k in range(TOP_K):
                    dst = dr_ref[pair0 + r * TOP_K + k]
                    pltpu.make_async_copy(src_ref.at[pl.ds(r * ROW_TILES, ROW_TILES), :],
                                          xs_hbm.at[pl.ds(dst * ROW_TILES, ROW_TILES), :], sem).start(priority=k)
            return c

        lax.fori_loop(0, n_tok // unroll, body, 0)
        for _ in range(TOP_K):
            pltpu.make_async_copy(src_ref, xs_hbm.at[pl.ds(0, n_tok * ROW_TILES), :], sem).wait()

    @pl.when(i < p_steps)
    def _():
        scatter(hp_ref, tile_p, i * (tile_p * TOP_K))

    @pl.when(i == p_steps)
    def _():
        scatter(hs_ref, n_sample, p_steps * tile_p * TOP_K)
        zero_s[...] = jnp.zeros(zero_s.shape, F32)

        def pad_body(t, carry):
            n = tr_ref[t]

            @pl.when(n < MOE_TM)
            def _():
                rows = (MOE_TM - n) * ROW_TILES
                cp = pltpu.make_async_copy(zero_s.at[pl.ds(0, rows), :],
                                           xs_hbm.at[pl.ds((t * MOE_TM + n) * ROW_TILES, rows), :], zsem)
                cp.start()
                cp.wait()
            return carry

        lax.fori_loop(0, n_tiles, pad_body, 0)


def _dispatch(dest_row, tile_rows, h_lin_p, h_lin_s, n_tiles):
    n_prompt, n_sample = h_lin_p.shape[0] // ROW_TILES, h_lin_s.shape[0] // ROW_TILES
    tile_p = min(512, n_prompt)
    p_steps = n_prompt // tile_p
    return pl.pallas_call(
        functools.partial(_dispatch_kernel, tile_p=tile_p, p_steps=p_steps, n_sample=n_sample, n_tiles=n_tiles),
        grid_spec=pltpu.PrefetchScalarGridSpec(
            num_scalar_prefetch=2,
            grid=(p_steps + 1,),
            in_specs=[pl.BlockSpec((tile_p * ROW_TILES, LANES), lambda i, dr, tr: (jnp.minimum(i, p_steps - 1), 0)),
                      pl.BlockSpec((n_sample * ROW_TILES, LANES), lambda i, dr, tr: (0, 0))],
            out_specs=pl.BlockSpec(memory_space=pl.ANY),
            scratch_shapes=[pltpu.VMEM((MOE_TM * ROW_TILES, LANES), F32), pltpu.SemaphoreType.DMA(()),
                            pltpu.SemaphoreType.DMA(())]),
        out_shape=jax.ShapeDtypeStruct((n_tiles * MOE_TM * ROW_TILES, LANES), F32),
        compiler_params=_cparams(("arbitrary",)),
        name="moe_dispatch",
    )(dest_row, tile_rows, h_lin_p, h_lin_s)


def _moe_kernel(te_ref, tr_ref, nu_ref, x_ref, wg_ref, wu_ref, wd_ref, y_ref, wg_s, wu_s, wd_s):
    i = pl.program_id(0)
    n = tr_ref[i]

    @pl.when(n == 0)
    def _():
        y_ref[...] = jnp.zeros(y_ref.shape, F32)

    @pl.when(n > 0)
    def _():
        changed = jnp.logical_or(i == 0, te_ref[i] != te_ref[jnp.maximum(i - 1, 0)])

        @pl.when(changed)
        def _():
            wg_s[...] = wg_ref[0].astype(BF16)
            wu_s[...] = wu_ref[0].astype(BF16)
            wd_s[...] = wd_ref[0].astype(BF16)

        x = _rows_to_tile(x_ref, 0, MOE_TM, ROW_TILES).astype(BF16)
        hg = jnp.dot(x, wg_s[...], preferred_element_type=F32)
        hu = jnp.dot(x, wu_s[...], preferred_element_type=F32)
        act = ((hg * jax.nn.sigmoid(hg)) * hu).astype(BF16)
        y = jnp.dot(act, wd_s[...], preferred_element_type=F32)
        _store_row_linear(y_ref, y, 0, ROW_TILES)


def _moe(x_sorted, tile_expert, tile_rows, n_used, w_gate, w_up, w_down, n_tiles):
    wmap = lambda i, te, tr, nu: (te[i], 0, 0)
    rows = MOE_TM * ROW_TILES
    return pl.pallas_call(
        _moe_kernel,
        grid_spec=pltpu.PrefetchScalarGridSpec(
            num_scalar_prefetch=3,
            grid=(n_tiles,),
            in_specs=[pl.BlockSpec((rows, LANES), lambda i, te, tr, nu: (jnp.minimum(i, nu[0] - 1), 0)),
                      pl.BlockSpec((1, D_MODEL, D_FF), wmap), pl.BlockSpec((1, D_MODEL, D_FF), wmap),
                      pl.BlockSpec((1, D_FF, D_MODEL), wmap)],
            out_specs=pl.BlockSpec((rows, LANES), lambda i, te, tr, nu: (i, 0)),
            scratch_shapes=[pltpu.VMEM((D_MODEL, D_FF), BF16), pltpu.VMEM((D_MODEL, D_FF), BF16),
                            pltpu.VMEM((D_FF, D_MODEL), BF16)]),
        out_shape=jax.ShapeDtypeStruct((n_tiles * rows, LANES), F32),
        compiler_params=_cparams(("arbitrary",)),
        name="moe_experts",
    )(tile_expert, tile_rows, n_used, x_sorted, w_gate, w_up, w_down)


def _combine_kernel(dr_ref, ys_hbm, x_ref, wt_ref, o_ref, g_s, sem, *, tm):
    i = pl.program_id(0)
    nt = pl.num_programs(0)
    slot = i % 2
    unroll = 8

    def issue(tile, dst_slot):
        def body(b, c):
            for u in range(unroll):
                r = b * unroll + u
                for k in range(TOP_K):
                    row = dr_ref[(tile * tm + r) * TOP_K + k]
                    pltpu.make_async_copy(ys_hbm.at[pl.ds(row * ROW_TILES, ROW_TILES), :],
                                          g_s.at[dst_slot, pl.ds((k * tm + r) * ROW_TILES, ROW_TILES), :],
                                          sem.at[dst_slot]).start(priority=k)
            return c
        lax.fori_loop(0, tm // unroll, body, 0)

    @pl.when(i == 0)
    def _():
        issue(0, 0)

    @pl.when(i + 1 < nt)
    def _():
        issue(i + 1, 1 - slot)

    pltpu.make_async_copy(ys_hbm.at[pl.ds(0, TOP_K * tm * ROW_TILES), :], g_s.at[slot], sem.at[slot]).wait()
    wt = wt_ref[...]
    y0 = _rows_to_tile(g_s.at[slot], 0, tm, ROW_TILES)
    y1 = _rows_to_tile(g_s.at[slot], tm * ROW_TILES, tm, ROW_TILES)
    o_ref[...] = x_ref[...] + (wt[:, 0:1] * y0 + wt[:, 1:2] * y1)


def _combine(dest_row, y_sorted, x1, wt, tm):
    t = x1.shape[0]
    tile = lambda i, dr: (i, 0)
    return pl.pallas_call(
        functools.partial(_combine_kernel, tm=tm),
        grid_spec=pltpu.PrefetchScalarGridSpec(
            num_scalar_prefetch=1,
            grid=(t // tm,),
            in_specs=[pl.BlockSpec(memory_space=pl.ANY), pl.BlockSpec((tm, D_MODEL), tile),
                      pl.BlockSpec((tm, LANES), tile)],
            out_specs=pl.BlockSpec((tm, D_MODEL), tile),
            scratch_shapes=[pltpu.VMEM((2, TOP_K * tm * ROW_TILES, LANES), F32), pltpu.SemaphoreType.DMA((2,))]),
        out_shape=jax.ShapeDtypeStruct((t, D_MODEL), F32),
        compiler_params=_cparams(("arbitrary",)),
        name="moe_combine",
    )(dest_row, y_sorted, x1, wt)


def _rotary_tables(pos):
    inv_freq = ROPE_THETA ** (-jnp.arange(0, ROT_DIM, 2, dtype=F32) / ROT_DIM)
    ang = pos[:, None] * inv_freq[None, :]
    cos, sin = jnp.cos(ang), jnp.sin(ang)
    rest = LANES - ROT_DIM
    ones = jnp.ones((pos.shape[0], rest), F32)
    zeros = jnp.zeros((pos.shape[0], rest), F32)
    return jnp.concatenate([cos, cos, ones], axis=1), jnp.concatenate([-sin, sin, zeros], axis=1)


def _projections(x, pos, norm1_g, w_in, q_norm_g, k_norm_g):
    t = x.shape[0]
    tm = min(TM_MAX, t)
    h = _rmsnorm(x, norm1_g, tm)
    tcos, tsin = _rotary_tables(pos)
    c_q = 2 * D_CONV
    c_k = c_q + QK_WIDTH
    c_v = c_k + QK_WIDTH
    c_g = c_v + V_WIDTH
    group = jnp.arange(TN) // HEAD_DIM
    group_ones = (group[:, None] == group[None, :]).astype(BF16)
    rot_in = lambda g: [g.reshape(1, LANES), tcos, tsin, group_ones]
    rot_specs = lambda order: [pl.BlockSpec((1, LANES), lambda a, b: (0, 0)),
                               pl.BlockSpec((tm, LANES), lambda a, b: ((a, b)[order], 0)),
                               pl.BlockSpec((tm, LANES), lambda a, b: ((a, b)[order], 0)),
                               pl.BlockSpec((TN, TN), lambda a, b: (0, 0))]
    u_lin = _glu_call(h, w_in, tm)
    q_b = _stationary_call(_q_kernel, h, w_in, c_q, QK_WIDTH, rot_in(q_norm_g), rot_specs(1), BF16, tm, "proj_q")
    k_lin, k_b = _row_linear_call(_k_kernel, h, w_in, c_k, QK_WIDTH, rot_in(k_norm_g), rot_specs(0), tm, "proj_k")
    v_lin, v_b = _row_linear_call(_v_kernel, h, w_in, c_v, V_WIDTH, [], [], tm, "proj_v")
    gates = _stationary_call(_gate_kernel, h, w_in, c_g, 2 * D_MODEL, [], [], F32, tm, "proj_gates")
    return u_lin, q_b, k_lin, k_b, v_lin, v_b, gates


def _layer(x_p, x_s, cache_k, cache_v, state_conv, page_table, lambda_init,
           norm1_g, w_in, conv_w, conv_b, conv_ln_g, conv_ln_b, w_conv_out, q_norm_g, k_norm_g,
           lambda_q1, lambda_k1, lambda_q2, lambda_k2, subln_g, w_attn_out, w_out, norm2_g,
           w_router_group, b_router_group, w_router_expert, b_router_expert, w_gate_e, w_up_e, w_down_e):
    batch, seq, _ = x_p.shape
    nb, n_new, _ = x_s.shape
    past = page_table.shape[1] * PAGE
    tp, ts = batch * seq, nb * n_new
    xp, xs = x_p.reshape(tp, D_MODEL), x_s.reshape(ts, D_MODEL)
    pos_p = jnp.tile(jnp.arange(seq, dtype=F32), batch)
    pos_s = jnp.tile(past + jnp.arange(n_new, dtype=F32), nb)
    u_p, q_p, kl_p, kb_p, vl_p, vb_p, gates_p = _projections(xp, pos_p, norm1_g, w_in, q_norm_g, k_norm_g)
    u_s, q_s, kl_s, kb_s, vl_s, vb_s, gates_s = _projections(xs, pos_s, norm1_g, w_in, q_norm_g, k_norm_g)

    u_p = u_p.reshape(batch, seq, CONV_TILES, LANES)
    u_s = u_s.reshape(nb, n_new, CONV_TILES, LANES)
    hist_p = jnp.concatenate([jnp.zeros((batch, HIST, CONV_TILES, LANES), F32), u_p], axis=1)
    hist_s = jnp.concatenate([state_conv.reshape(nb, HIST, CONV_TILES, LANES), u_s], axis=1)
    conv = lambda hist, name: _conv_branch(hist, conv_w, conv_b, conv_ln_g, conv_ln_b, name)
    yc_p = conv(hist_p, "conv_prompt").reshape(tp * CONV_TILES, LANES)
    yc_s = conv(hist_s, "conv_sample").reshape(ts * CONV_TILES, LANES)

    lam_rows = jnp.stack([lambda_q1, lambda_k1, lambda_q2, lambda_k2]).astype(F32)
    o_p = _prompt_attention(q_p, kb_p, vb_p, lam_rows, subln_g, batch, seq, lambda_init)
    pad_rows = lambda a: jnp.pad(a.reshape(nb, n_new, -1), ((0, 0), (0, NEW_ROWS - n_new), (0, 0)))
    o_s = _sample_attention(_query_rows(q_s.reshape(nb, n_new, QK_WIDTH)), pad_rows(kb_s), pad_rows(vb_s),
                            cache_k, cache_v, page_table, lam_rows, subln_g, lambda_init, n_new)
    o_s = o_s[:, :n_new].reshape(ts, V_WIDTH).astype(BF16)

    w_r = jnp.zeros((D_MODEL, LANES), F32).at[:, :N_GROUPS].set(w_router_group)
    w_r = w_r.at[:, N_GROUPS:N_GROUPS + N_EXPERTS].set(w_router_expert)
    w_hi = w_r.astype(BF16)
    w_lo = (w_r - w_hi.astype(F32)).astype(BF16)
    w_split = jnp.concatenate([w_hi, w_lo], axis=1)
    b_r = jnp.zeros((1, LANES), F32).at[0, :N_GROUPS].set(b_router_group)
    b_r = b_r.at[0, N_GROUPS:N_GROUPS + N_EXPERTS].set(b_router_expert)

    def tail(x, yc, o_n, gates):
        tm = min(TM_MAX, x.shape[0])
        m = _merge(yc, o_n, w_conv_out, w_attn_out, gates, tm)
        x1 = _out_projection(m, w_out, x, tm)
        return (x1,) + tuple(_router(x1, norm2_g, w_split, b_r, min(512, x.shape[0])))

    x1_p, hl_p, e_p, wt_p = tail(xp, yc_p, o_p, gates_p)
    x1_s, hl_s, e_s, wt_s = tail(xs, yc_s, o_s, gates_s)

    n_pairs = (tp + ts) * TOP_K
    n_tiles = -(-n_pairs // MOE_TM) + N_EXPERTS
    e_all = jnp.concatenate([e_p[:, :TOP_K], e_s[:, :TOP_K]], axis=0)
    dest_row, te, tile_rows, n_used = _routing_plan(e_all, n_tiles)
    x_sorted = _dispatch(dest_row, tile_rows, hl_p, hl_s, n_tiles)
    y_sorted = _moe(x_sorted, te, tile_rows, n_used, w_gate_e, w_up_e, w_down_e, n_tiles)
    y_p = _combine(dest_row[:tp * TOP_K], y_sorted, x1_p, wt_p, CMB_TM)
    y_s = _combine(dest_row[tp * TOP_K:], y_sorted, x1_s, wt_s, min(CMB_TM, ts))

    v_out = lambda a, lead: a.reshape(lead + (2, N_HEADS, LANES)).swapaxes(-3, -2).reshape(lead + (N_HEADS, V_DIM))
    return (y_p.reshape(batch, seq, D_MODEL), y_s.reshape(nb, n_new, D_MODEL),
            kl_p.reshape(batch, seq, N_HEADS, 2, HEAD_DIM), v_out(vl_p, (batch, seq)),
            hist_p[:, seq:].reshape(batch, HIST, D_CONV),
            kl_s.reshape(nb, n_new, N_HEADS, 2, HEAD_DIM), v_out(vl_s, (nb, n_new)),
            hist_s[:, n_new:].reshape(nb, HIST, D_CONV))


def kernel(x_prompt, x_sample, cache_k, cache_v, state_conv, page_table, norm1_g, w_in, conv_w, conv_b, conv_ln_g, conv_ln_b, w_conv_out, q_norm_g, k_norm_g, lambda_q1, lambda_k1, lambda_q2, lambda_k2, subln_g, w_attn_out, w_out, norm2_g, w_router_group, b_router_group, w_router_expert, b_router_expert, w_gate_e, w_up_e, w_down_e):
    depth = norm1_g.shape[0]
    x_p, x_s = x_prompt, x_sample
    outs = [[] for _ in range(6)]
    for l in range(depth):
        lambda_init = 0.8 - 0.6 * math.exp(-0.3 * l)
        x_p, x_s, *rest = _layer(
            x_p, x_s, cache_k[l], cache_v[l], state_conv[l], page_table, lambda_init,
            norm1_g[l], w_in[l], conv_w[l], conv_b[l], conv_ln_g[l], conv_ln_b[l], w_conv_out[l],
            q_norm_g[l], k_norm_g[l], lambda_q1[l], lambda_k1[l], lambda_q2[l], lambda_k2[l], subln_g[l],
            w_attn_out[l], w_out[l], norm2_g[l], w_router_group[l], b_router_group[l],
            w_router_expert[l], b_router_expert[l], w_gate_e[l], w_up_e[l], w_down_e[l])
        for acc, val in zip(outs, rest):
            acc.append(val)
    return (x_p, x_s) + tuple(jnp.stack(o) for o in outs)
```

```python
import functools
import math

import jax
import jax.numpy as jnp
from jax import lax
from jax.experimental import pallas as pl
from jax.experimental.pallas import tpu as pltpu

F32 = jnp.float32
BF16 = jnp.bfloat16

D_MODEL = 2048
N_HEADS = 8
HEAD_DIM = 128
V_DIM = 2 * HEAD_DIM
QK_WIDTH = N_HEADS * 2 * HEAD_DIM
V_WIDTH = N_HEADS * V_DIM
ROT_DIM = HEAD_DIM // 4
ROPE_THETA = 500000.0
ATTN_SCALE = HEAD_DIM ** -0.5
LOG2E = math.log2(math.e)
NEG_INF = -1e30
D_CONV = D_MODEL // 2
CONV_WIDTH = 31
HIST = CONV_WIDTH - 1
N_GROUPS = 4
EXPERTS_PER_GROUP = 8
N_EXPERTS = N_GROUPS * EXPERTS_PER_GROUP
TOP_K = 2
D_FF = D_MODEL // 4
EPS = 1e-6
PAGE = 128

LANES = 128
SUBLANES = 8
ROW_TILES = D_MODEL // LANES
CONV_TILES = D_CONV // LANES
VMEM_LIMIT = 56 * 1024 * 1024

TM_MAX = 1024
TN = 512
TN_WIDE = 1024
ATTN_ROWS = 256
PAGES_PER_STEP = 8
QROWS = 2 * 4 * SUBLANES
NEW_ROWS = 16
MOE_TM = 256
CMB_TM = 256
CONV_ROWS = 32


def _cparams(sem):
    return pltpu.CompilerParams(dimension_semantics=sem, vmem_limit_bytes=VMEM_LIMIT)


def _rows_to_tile(buf, base, rows, tiles):
    return jnp.concatenate([buf[pl.ds(base + c, rows, stride=tiles), :] for c in range(tiles)], axis=1)


def _rmsnorm_kernel(x_ref, g_ref, o_ref):
    x = x_ref[...]
    ms = jnp.mean(x * x, axis=-1, keepdims=True)
    o_ref[...] = ((x * lax.rsqrt(ms + EPS)) * g_ref[...]).astype(o_ref.dtype)


def _rmsnorm(x, g, tm):
    t, d = x.shape
    return pl.pallas_call(
        _rmsnorm_kernel,
        grid=(t // tm,),
        in_specs=[pl.BlockSpec((tm, d), lambda i: (i, 0)), pl.BlockSpec((1, d), lambda i: (0, 0))],
        out_specs=pl.BlockSpec((tm, d), lambda i: (i, 0)),
        out_shape=jax.ShapeDtypeStruct((t, d), BF16),
        compiler_params=_cparams(("arbitrary",)),
        name="rmsnorm1",
    )(x, g.reshape(1, d))


def _cache_weight(w_ref, w_s):
    @pl.when(pl.program_id(1) == 0)
    def _():
        w_s[...] = w_ref[...].astype(BF16)


def _store_row_linear(o_ref, val, first_tile, tiles, row_of_tile=None):
    rows = val.shape[0]
    for c in range(val.shape[1] // LANES):
        tile = first_tile + c
        dst = tile if row_of_tile is None else row_of_tile(tile)
        o_ref[pl.ds(dst, rows, stride=tiles), :] = val[:, c * LANES:(c + 1) * LANES]


def _glu_kernel(h_ref, wa_ref, wb_ref, u_ref):
    h = h_ref[...]
    a = jnp.dot(h, wa_ref[...].astype(BF16), preferred_element_type=F32)
    b = jnp.dot(h, wb_ref[...].astype(BF16), preferred_element_type=F32)
    _store_row_linear(u_ref, a * jax.nn.sigmoid(b), pl.program_id(1) * (TN // LANES), CONV_TILES)


def _glu_call(h, w, tm):
    t, k = h.shape
    nj = D_CONV // TN
    return pl.pallas_call(
        _glu_kernel,
        grid=(t // tm, nj),
        in_specs=[pl.BlockSpec((tm, k), lambda i, j: (i, 0)), pl.BlockSpec((k, TN), lambda i, j: (0, j)),
                  pl.BlockSpec((k, TN), lambda i, j: (0, nj + j))],
        out_specs=pl.BlockSpec((tm * CONV_TILES, LANES), lambda i, j: (i, 0)),
        out_shape=jax.ShapeDtypeStruct((t * CONV_TILES, LANES), F32),
        compiler_params=_cparams(("arbitrary", "arbitrary")),
        name="proj_glu",
    )(h, w, w)


def _head_norm_rotary(acc, ones_ref, gain, tcos, tsin):
    ssq = jnp.dot((acc * acc).astype(BF16), ones_ref[...], preferred_element_type=F32)
    tn = acc * lax.rsqrt(ssq * (1.0 / HEAD_DIM) + EPS)
    lane = lax.broadcasted_iota(jnp.int32, (acc.shape[0], LANES), 1)
    half = ROT_DIM // 2
    outs = []
    for g in range(acc.shape[1] // LANES):
        t = tn[:, g * LANES:(g + 1) * LANES] * gain
        partner = jnp.where(lane < half, pltpu.roll(t, LANES - half, 1), pltpu.roll(t, half, 1))
        outs.append(t * tcos + partner * tsin)
    return jnp.concatenate(outs, axis=1)


def _q_kernel(h_ref, w_ref, g_ref, tc_ref, ts_ref, ones_ref, ob_ref, w_s):
    _cache_weight(w_ref, w_s)
    acc = jnp.dot(h_ref[...], w_s[...], preferred_element_type=F32)
    y = _head_norm_rotary(acc, ones_ref, g_ref[...], tc_ref[...], ts_ref[...])
    ob_ref[...] = (y * (ATTN_SCALE * LOG2E)).astype(BF16)


def _gate_kernel(h_ref, w_ref, o_ref, w_s):
    _cache_weight(w_ref, w_s)
    o_ref[...] = jax.nn.sigmoid(jnp.dot(h_ref[...], w_s[...], preferred_element_type=F32)).astype(o_ref.dtype)


def _stationary_call(kernel, h, w, col0, width, extra_in, extra_specs, out_dtype, tm, tn, name):
    t, k = h.shape
    return pl.pallas_call(
        kernel,
        grid=(width // tn, t // tm),
        in_specs=[pl.BlockSpec((tm, k), lambda j, i: (i, 0)),
                  pl.BlockSpec((k, tn), lambda j, i: (0, col0 // tn + j))] + extra_specs,
        out_specs=pl.BlockSpec((tm, tn), lambda j, i: (i, j)),
        out_shape=jax.ShapeDtypeStruct((t, width), out_dtype),
        scratch_shapes=[pltpu.VMEM((k, tn), BF16)],
        compiler_params=_cparams(("arbitrary", "arbitrary")),
        name=name,
    )(h, w, *extra_in)


def _k_kernel(h_ref, w_ref, g_ref, tc_ref, ts_ref, ones_ref, of_ref, ob_ref):
    acc = jnp.dot(h_ref[...], w_ref[...].astype(BF16), preferred_element_type=F32)
    y = _head_norm_rotary(acc, ones_ref, g_ref[...], tc_ref[...], ts_ref[...])
    _store_row_linear(of_ref, y, pl.program_id(1) * (TN // LANES), ROW_TILES)
    ob_ref[...] = y.astype(BF16)


def _v_kernel(h_ref, w_ref, of_ref, ob_ref):
    acc = jnp.dot(h_ref[...], w_ref[...].astype(BF16), preferred_element_type=F32)
    _store_row_linear(of_ref, acc, pl.program_id(1) * (TN // LANES), ROW_TILES,
                      row_of_tile=lambda tile: (tile % 2) * N_HEADS + tile // 2)
    ob_ref[...] = acc.astype(BF16)


def _row_linear_call(kernel, h, w, col0, width, extra_in, extra_specs, tm, name):
    t, k = h.shape
    tiles = width // LANES
    return pl.pallas_call(
        kernel,
        grid=(t // tm, width // TN),
        in_specs=[pl.BlockSpec((tm, k), lambda i, j: (i, 0)),
                  pl.BlockSpec((k, TN), lambda i, j: (0, col0 // TN + j))] + extra_specs,
        out_specs=[pl.BlockSpec((tm * tiles, LANES), lambda i, j: (i, 0)), pl.BlockSpec((tm, TN), lambda i, j: (i, j))],
        out_shape=[jax.ShapeDtypeStruct((t * tiles, LANES), F32), jax.ShapeDtypeStruct((t, width), BF16)],
        compiler_params=_cparams(("arbitrary", "arbitrary")),
        name=name,
    )(h, w, *extra_in)


def _conv_kernel(hist_ref, w_ref, b_ref, g_ref, beta_ref, o_ref, *, length):
    rows = min(CONV_ROWS, length)

    def step(i, carry):
        t0 = i * rows
        acc = jnp.zeros((rows, CONV_TILES, LANES), F32)
        for j in range(CONV_WIDTH):
            acc = acc + hist_ref[0, pl.ds(t0 + j, rows)] * w_ref[j]
        y = acc + b_ref[...]
        mu = jnp.mean(y, axis=(1, 2), keepdims=True)
        yc = y - mu
        var = jnp.mean(yc * yc, axis=(1, 2), keepdims=True)
        z = yc * lax.rsqrt(var + EPS) * g_ref[...] + beta_ref[...]
        o_ref[0, pl.ds(t0, rows)] = z * jax.nn.sigmoid(z)
        return carry

    trips = length // rows
    lax.fori_loop(0, trips, step, 0, unroll=2 if trips % 2 == 0 else 1)


def _conv_branch(hist, conv_w, conv_b, ln_g, ln_b, name):
    b, lh = hist.shape[:2]
    length = lh - HIST
    tile = lambda a: a.reshape(-1, CONV_TILES, LANES)
    whole = lambda n: pl.BlockSpec((n, CONV_TILES, LANES), lambda i: (0, 0, 0))
    return pl.pallas_call(
        functools.partial(_conv_kernel, length=length),
        grid=(b,),
        in_specs=[pl.BlockSpec((1, lh, CONV_TILES, LANES), lambda i: (i, 0, 0, 0)),
                  whole(CONV_WIDTH), whole(1), whole(1), whole(1)],
        out_specs=pl.BlockSpec((1, length, CONV_TILES, LANES), lambda i: (i, 0, 0, 0)),
        out_shape=jax.ShapeDtypeStruct((b, length, CONV_TILES, LANES), F32),
        compiler_params=_cparams(("arbitrary",)),
        name=name,
    )(hist, tile(conv_w), tile(conv_b), tile(ln_g), tile(ln_b))


def _lambda_value(lam_ref, lambda_init):
    v = lam_ref[...]
    s1 = jnp.sum(v[0:1] * v[1:2], axis=-1, keepdims=True)
    s2 = jnp.sum(v[2:3] * v[3:4], axis=-1, keepdims=True)
    return jnp.exp(s1) - jnp.exp(s2) + lambda_init


def _prompt_attn_kernel(q_ref, k_ref, v_ref, lam_ref, sg_ref, o_ref, *, lambda_init, seq):
    rb = ATTN_ROWS
    lam = _lambda_value(lam_ref, lambda_init)
    row = lax.broadcasted_iota(jnp.int32, (rb, rb), 0)
    col = lax.broadcasted_iota(jnp.int32, (rb, rb), 1)
    for r in range(seq // rb):
        rows = pl.ds(r * rb, rb)
        kc = (r + 1) * rb
        v = v_ref[0:kc, :]
        outs = []
        for c in range(2):
            q = q_ref[rows, c * HEAD_DIM:(c + 1) * HEAD_DIM]
            k = k_ref[0:kc, c * HEAD_DIM:(c + 1) * HEAD_DIM]
            s = lax.dot_general(q, k, (((1,), (1,)), ((), ())), preferred_element_type=F32)
            diag = jnp.where(col <= row, s[:, kc - rb:], NEG_INF)
            s = diag if r == 0 else jnp.concatenate([s[:, :kc - rb], diag], axis=1)
            p = jnp.exp2(s - jnp.max(s, axis=-1, keepdims=True))
            l = jnp.sum(p, axis=-1, keepdims=True)
            outs.append(jnp.dot(p.astype(BF16), v, preferred_element_type=F32) / l)
        o = outs[0] - lam * outs[1]
        ms = jnp.mean(o * o, axis=-1, keepdims=True)
        o_ref[rows, :] = (((o * lax.rsqrt(ms + EPS)) * sg_ref[...]) * (1.0 - lambda_init)).astype(o_ref.dtype)


def _prompt_attention(q, k, v, lam_rows, subln_g, batch, seq, lambda_init):
    blk = lambda b, h: (b, h)
    return pl.pallas_call(
        functools.partial(_prompt_attn_kernel, lambda_init=lambda_init, seq=seq),
        grid=(batch, N_HEADS),
        in_specs=[pl.BlockSpec((seq, V_DIM), blk), pl.BlockSpec((seq, V_DIM), blk), pl.BlockSpec((seq, V_DIM), blk),
                  pl.BlockSpec((4, HEAD_DIM), lambda b, h: (0, 0)), pl.BlockSpec((1, V_DIM), lambda b, h: (0, 0))],
        out_specs=pl.BlockSpec((seq, V_DIM), blk),
        out_shape=jax.ShapeDtypeStruct((batch * seq, V_WIDTH), BF16),
        compiler_params=_cparams(("arbitrary", "arbitrary")),
        name="prompt_attention",
    )(q, k, v, lam_rows, subln_g.reshape(1, V_DIM))


def _page_tiles(x):
    tiles = jnp.swapaxes(x, 0, 1)
    return [tiles[i].astype(BF16) for i in range(SUBLANES)]


def _sample_attn_kernel(pt_ref, *refs, n_new, lambda_init):
    g = PAGES_PER_STEP
    k_refs, v_refs = refs[:g], refs[g:2 * g]
    q_ref, kn_ref, vn_ref, lam_ref, sg_ref, o_ref, m_s, l_s, acc_s = refs[2 * g:]
    j = pl.program_id(1)
    q2 = q_ref[0]
    nt_dims = (((1,), (1,)), ((), ()))

    @pl.when(j == 0)
    def _():
        m_s[...] = jnp.full(m_s.shape, NEG_INF, F32)
        l_s[...] = jnp.zeros(l_s.shape, F32)
        acc_s[...] = jnp.zeros(acc_s.shape, F32)

    def online_update(s, values):
        m_old = m_s[...]
        m_new = jnp.maximum(m_old, jnp.max(s, axis=-1, keepdims=True))
        alpha = jnp.exp2(m_old - m_new)
        p = jnp.exp2(s - m_new)
        l_s[...] = alpha * l_s[...] + jnp.sum(p, axis=-1, keepdims=True)
        pv = None
        c0 = 0
        for n, v in values:
            part = jnp.dot(p[:, c0:c0 + n].astype(BF16), v, preferred_element_type=F32)
            pv = part if pv is None else pv + part
            c0 += n
        acc_s[...] = alpha * acc_s[...] + pv
        m_s[...] = m_new

    s_parts, values = [], []
    for p in range(g):
        kfull = jnp.concatenate(_page_tiles(k_refs[p][:, 0]) + _page_tiles(k_refs[p][:, 1]), axis=1)
        s_parts.append(lax.dot_general(q2, kfull, nt_dims, preferred_element_type=F32))
        t0, t1 = _page_tiles(v_refs[p][:, :, :LANES]), _page_tiles(v_refs[p][:, :, LANES:])
        values.append((PAGE, jnp.concatenate([t for pair in zip(t0, t1) for t in pair], axis=1)))
    online_update(jnp.concatenate(s_parts, axis=1), values)

    @pl.when(j == pl.num_programs(1) - 1)
    def _():
        kn = kn_ref[0]
        s = lax.dot_general(q2, kn, nt_dims, preferred_element_type=F32)
        key = lax.broadcasted_iota(jnp.int32, s.shape, 1)
        qry = (lax.broadcasted_iota(jnp.int32, s.shape, 0) % (4 * SUBLANES)) // SUBLANES
        s = jnp.where((key <= qry) & (key < n_new), s, NEG_INF)
        online_update(s, [(kn.shape[0], vn_ref[0])])
        lam = _lambda_value(lam_ref, lambda_init)
        half = QROWS // 2
        acc, l = acc_s[...], l_s[...]
        o_all = acc[:half] / l[:half] - lam * (acc[half:] / l[half:])
        head = lax.broadcasted_iota(jnp.int32, (SUBLANES, V_WIDTH), 0)
        own = head == lax.broadcasted_iota(jnp.int32, (SUBLANES, V_WIDTH), 1) // V_DIM
        rows = []
        for q in range(n_new):
            o = jnp.where(own, o_all[q * SUBLANES:(q + 1) * SUBLANES, :], 0.0)
            ms = jnp.sum(o * o, axis=-1, keepdims=True) / V_DIM
            o = jnp.sum(o * lax.rsqrt(ms + EPS), axis=0, keepdims=True)
            rows.append((o * sg_ref[...]) * (1.0 - lambda_init))
        rows.append(jnp.zeros((SUBLANES - n_new, V_WIDTH), F32))
        o_ref[0] = jnp.concatenate(rows, axis=0)


def _sample_attention(q_rows, k_new, v_new, cache_k, cache_v, page_table, lam_rows, subln_g, lambda_init, n_new):
    nb, n_pages = page_table.shape
    g = PAGES_PER_STEP
    n_pool = cache_k.shape[0]
    ck = cache_k.reshape(n_pool * PAGE, 2, SUBLANES, HEAD_DIM)
    cv = cache_v.reshape(n_pool * PAGE, N_HEADS, V_DIM)
    pt = page_table.reshape(-1)
    kmap = lambda p: (lambda b, j, pt: (pt[b * n_pages + j * g + p], 0, 0, 0))
    vmap = lambda p: (lambda b, j, pt: (pt[b * n_pages + j * g + p], 0, 0))
    k_specs = [pl.BlockSpec((PAGE, 2, SUBLANES, HEAD_DIM), kmap(p)) for p in range(g)]
    v_specs = [pl.BlockSpec((PAGE, N_HEADS, V_DIM), vmap(p)) for p in range(g)]
    new_rows = k_new.shape[1]
    per_b = lambda shape: pl.BlockSpec((1,) + shape, lambda b, j, pt: (b, 0, 0))
    const = lambda shape: pl.BlockSpec(shape, lambda b, j, pt: (0, 0))
    tiled_g = jnp.tile(subln_g.reshape(1, V_DIM), (1, N_HEADS))
    return pl.pallas_call(
        functools.partial(_sample_attn_kernel, n_new=n_new, lambda_init=lambda_init),
        grid_spec=pltpu.PrefetchScalarGridSpec(
            num_scalar_prefetch=1,
            grid=(nb, n_pages // g),
            in_specs=k_specs + v_specs + [per_b((QROWS, QK_WIDTH)), per_b((new_rows, QK_WIDTH)),
                                          per_b((new_rows, V_WIDTH)), const((4, HEAD_DIM)), const((1, V_WIDTH))],
            out_specs=per_b((SUBLANES, V_WIDTH)),
            scratch_shapes=[pltpu.VMEM((QROWS, 1), F32), pltpu.VMEM((QROWS, 1), F32), pltpu.VMEM((QROWS, V_WIDTH), F32)]),
        out_shape=jax.ShapeDtypeStruct((nb, SUBLANES, V_WIDTH), F32),
        compiler_params=_cparams(("arbitrary", "arbitrary")),
        name="sample_attention",
    )(pt, *([ck] * g), *([cv] * g), q_rows, k_new, v_new, lam_rows, tiled_g)


def _query_rows(q_s):
    n_new = q_s.shape[1]
    r = jnp.arange(QROWS)
    r_c, r_q, r_h = r // (4 * SUBLANES), (r % (4 * SUBLANES)) // SUBLANES, r % SUBLANES
    col_hc = jnp.arange(QK_WIDTH) // HEAD_DIM
    keep = (col_hc[None, :] == (r_h * 2 + r_c)[:, None]) & (r_q < n_new)[:, None]
    picked = q_s[:, jnp.minimum(r_q, n_new - 1), :]
    return jnp.where(keep[None], picked, jnp.zeros((), q_s.dtype))


def _merge_kernel(yc_ref, ya_ref, wc_ref, wa_ref, gc_ref, ga_ref, m_ref, wc_s, wa_s):
    _cache_weight(wc_ref, wc_s)
    _cache_weight(wa_ref, wa_s)
    conv = _rows_to_tile(yc_ref, 0, ya_ref.shape[0], CONV_TILES).astype(BF16)
    yc = jnp.dot(conv, wc_s[...], preferred_element_type=F32)
    ya = jnp.dot(ya_ref[...], wa_s[...], preferred_element_type=F32)
    m_ref[...] = (gc_ref[...] * yc + ga_ref[...] * ya).astype(m_ref.dtype)


def _merge(yconv_lin, o_n, w_conv_out, w_attn_out, gates, tm):
    t = o_n.shape[0]
    tn = TN_WIDE
    nj, ni = D_MODEL // tn, t // tm
    return pl.pallas_call(
        _merge_kernel,
        grid=(nj, ni),
        in_specs=[pl.BlockSpec((tm * CONV_TILES, LANES), lambda j, i: (i, 0)),
                  pl.BlockSpec((tm, V_WIDTH), lambda j, i: (i, 0)),
                  pl.BlockSpec((D_CONV, tn), lambda j, i: (0, j)), pl.BlockSpec((V_WIDTH, tn), lambda j, i: (0, j)),
                  pl.BlockSpec((tm, tn), lambda j, i: (i, j)), pl.BlockSpec((tm, tn), lambda j, i: (i, nj + j))],
        out_specs=pl.BlockSpec((tm, tn), lambda j, i: (i, j)),
        out_shape=jax.ShapeDtypeStruct((t, D_MODEL), BF16),
        scratch_shapes=[pltpu.VMEM((D_CONV, tn), BF16), pltpu.VMEM((V_WIDTH, tn), BF16)],
        compiler_params=_cparams(("arbitrary", "arbitrary")),
        name="merge",
    )(yconv_lin, o_n, w_conv_out, w_attn_out, gates, gates)


def _residual_kernel(m_ref, w_ref, x_ref, o_ref, w_s):
    _cache_weight(w_ref, w_s)
    o_ref[...] = x_ref[...] + jnp.dot(m_ref[...], w_s[...], preferred_element_type=F32)


def _out_projection(m, w_out, x, tm):
    t = m.shape[0]
    tn = TN_WIDE
    return pl.pallas_call(
        _residual_kernel,
        grid=(D_MODEL // tn, t // tm),
        in_specs=[pl.BlockSpec((tm, D_MODEL), lambda j, i: (i, 0)), pl.BlockSpec((D_MODEL, tn), lambda j, i: (0, j)),
                  pl.BlockSpec((tm, tn), lambda j, i: (i, j))],
        out_specs=pl.BlockSpec((tm, tn), lambda j, i: (i, j)),
        out_shape=jax.ShapeDtypeStruct((t, D_MODEL), F32),
        scratch_shapes=[pltpu.VMEM((D_MODEL, tn), BF16)],
        compiler_params=_cparams(("arbitrary", "arbitrary")),
        name="out_projection",
    )(m, w_out, x)


def _router_kernel(x_ref, g_ref, w_ref, b_ref, hl_ref, e_ref, wt_ref):
    x = x_ref[...]
    ms = jnp.mean(x * x, axis=-1, keepdims=True)
    h = (x * lax.rsqrt(ms + EPS)) * g_ref[...]
    _store_row_linear(hl_ref, h, 0, ROW_TILES)
    h_hi = h.astype(BF16)
    h_lo = (h - h_hi.astype(F32)).astype(BF16)
    both = jnp.dot(h_hi, w_ref[...], preferred_element_type=F32)
    logits = both[:, :LANES] + both[:, LANES:] + jnp.dot(h_lo, w_ref[:, :LANES], preferred_element_type=F32)
    logits = logits + b_ref[...]
    lane = lax.broadcasted_iota(jnp.int32, logits.shape, 1)
    neg = -jnp.inf
    big = jnp.int32(LANES)
    lg = jnp.where(lane < N_GROUPS, logits, neg)
    eg = jnp.exp(lg - jnp.max(lg, axis=-1, keepdims=True))
    pg = eg / jnp.sum(eg, axis=-1, keepdims=True)
    p_top = jnp.max(pg, axis=-1, keepdims=True)
    g_idx = jnp.min(jnp.where(pg == p_top, lane, big), axis=-1, keepdims=True)
    in_group = (lane >= N_GROUPS) & (lane < N_GROUPS + N_EXPERTS) & (((lane - N_GROUPS) // EXPERTS_PER_GROUP) == g_idx)
    le = jnp.where(in_group, logits, neg)
    ee = jnp.exp(le - jnp.max(le, axis=-1, keepdims=True))
    pe = jnp.where(in_group, ee / jnp.sum(ee, axis=-1, keepdims=True), -1.0)
    w1 = jnp.max(pe, axis=-1, keepdims=True)
    i1 = jnp.min(jnp.where(pe == w1, lane, big), axis=-1, keepdims=True)
    pe2 = jnp.where(lane == i1, -1.0, pe)
    w2 = jnp.max(pe2, axis=-1, keepdims=True)
    i2 = jnp.min(jnp.where(pe2 == w2, lane, big), axis=-1, keepdims=True)
    denom = w1 + w2
    e_ref[...] = jnp.where(lane == 0, i1 - N_GROUPS, jnp.where(lane == 1, i2 - N_GROUPS, 0))
    wt_ref[...] = jnp.where(lane == 0, w1 / denom * p_top, jnp.where(lane == 1, w2 / denom * p_top, 0.0))


def _router(x1, norm2_g, w_router, b_router, tm):
    t = x1.shape[0]
    return pl.pallas_call(
        _router_kernel,
        grid=(t // tm,),
        in_specs=[pl.BlockSpec((tm, D_MODEL), lambda i: (i, 0)), pl.BlockSpec((1, D_MODEL), lambda i: (0, 0)),
                  pl.BlockSpec((D_MODEL, 2 * LANES), lambda i: (0, 0)), pl.BlockSpec((1, LANES), lambda i: (0, 0))],
        out_specs=[pl.BlockSpec((tm * ROW_TILES, LANES), lambda i: (i, 0)),
                   pl.BlockSpec((tm, LANES), lambda i: (i, 0)), pl.BlockSpec((tm, LANES), lambda i: (i, 0))],
        out_shape=[jax.ShapeDtypeStruct((t * ROW_TILES, LANES), F32),
                   jax.ShapeDtypeStruct((t, LANES), jnp.int32), jax.ShapeDtypeStruct((t, LANES), F32)],
        compiler_params=_cparams(("arbitrary",)),
        name="router",
    )(x1, norm2_g.reshape(1, D_MODEL), w_router, b_router)


def _routing_plan(expert_idx, n_tiles):
    e_flat = expert_idx.reshape(-1)
    onehot = (e_flat[:, None] == jnp.arange(N_EXPERTS, dtype=jnp.int32)[None, :]).astype(jnp.int32)
    csum = jnp.cumsum(onehot, axis=0)
    counts = csum[-1]
    rank = jnp.sum(onehot * csum, axis=1) - 1
    tiles_e = (counts + MOE_TM - 1) // MOE_TM
    tile_end = jnp.cumsum(tiles_e)
    tile_start = tile_end - tiles_e
    dest_row = jnp.sum(onehot * tile_start[None, :], axis=1) * MOE_TM + rank
    n_used = tile_end[-1]
    tile_id = jnp.arange(n_tiles, dtype=jnp.int32)
    te = jnp.sum((tile_end[None, :] <= jnp.minimum(tile_id, n_used - 1)[:, None]).astype(jnp.int32), axis=1)
    te = jnp.minimum(te, N_EXPERTS - 1)
    left = counts[te] - (tile_id - tile_start[te]) * MOE_TM
    tile_rows = jnp.where(tile_id < n_used, jnp.clip(left, 0, MOE_TM), 0)
    first = (tile_id < n_used) & (tile_id == tile_start[te])
    slot = (jnp.cumsum(first.astype(jnp.int32)) - 1) % 2
    ids = jnp.arange(N_EXPERTS, dtype=jnp.int32)
    later_used = (ids[None, :] > ids[:, None]) & (counts > 0)[None, :]
    next_used = jnp.min(jnp.where(later_used, ids[None, :], N_EXPERTS), axis=1)
    next_expert = jnp.where(next_used[te] < N_EXPERTS, next_used[te], -1)
    i32 = lambda a: a.astype(jnp.int32)
    moe_plan = (i32(te), i32(tile_rows), i32(n_used).reshape(1), i32(first), i32(slot), i32(next_expert))
    return i32(dest_row), moe_plan


def _dispatch_kernel(dr_ref, tr_ref, hp_ref, hs_ref, xs_hbm, zero_s, sem, zsem, *, tile_p, p_steps, n_sample, n_tiles):
    i = pl.program_id(0)
    unroll = 8

    def scatter(src_ref, n_tok, pair0):
        def body(b, c):
            for u in range(unroll):
                r = b * unroll + u
                for k in range(TOP_K):
                    dst = dr_ref[pair0 + r * TOP_K + k]
                    pltpu.make_async_copy(src_ref.at[pl.ds(r * ROW_TILES, ROW_TILES), :],
                                          xs_hbm.at[pl.ds(dst * ROW_TILES, ROW_TILES), :], sem).start(priority=k)
            return c

        lax.fori_loop(0, n_tok // unroll, body, 0)
        for _ in range(TOP_K):
            pltpu.make_async_copy(src_ref, xs_hbm.at[pl.ds(0, n_tok * ROW_TILES), :], sem).wait()

    @pl.when(i < p_steps)
    def _():
        scatter(hp_ref, tile_p, i * (tile_p * TOP_K))

    @pl.when(i == p_steps)
    def _():
        scatter(hs_ref, n_sample, p_steps * tile_p * TOP_K)
        zero_s[...] = jnp.zeros(zero_s.shape, F32)

        def pad_body(t, carry):
            n = tr_ref[t]

            @pl.when(n < MOE_TM)
            def _():
                rows = (MOE_TM - n) * ROW_TILES
                cp = pltpu.make_async_copy(zero_s.at[pl.ds(0, rows), :],
                                           xs_hbm.at[pl.ds((t * MOE_TM + n) * ROW_TILES, rows), :], zsem)
                cp.start()
                cp.wait()
            return carry

        lax.fori_loop(0, n_tiles, pad_body, 0)


def _dispatch(dest_row, tile_rows, h_lin_p, h_lin_s, n_tiles):
    n_prompt, n_sample = h_lin_p.shape[0] // ROW_TILES, h_lin_s.shape[0] // ROW_TILES
    tile_p = min(512, n_prompt)
    p_steps = n_prompt // tile_p
    return pl.pallas_call(
        functools.partial(_dispatch_kernel, tile_p=tile_p, p_steps=p_steps, n_sample=n_sample, n_tiles=n_tiles),
        grid_spec=pltpu.PrefetchScalarGridSpec(
            num_scalar_prefetch=2,
            grid=(p_steps + 1,),
            in_specs=[pl.BlockSpec((tile_p * ROW_TILES, LANES), lambda i, dr, tr: (jnp.minimum(i, p_steps - 1), 0)),
                      pl.BlockSpec((n_sample * ROW_TILES, LANES), lambda i, dr, tr: (0, 0))],
            out_specs=pl.BlockSpec(memory_space=pl.ANY),
            scratch_shapes=[pltpu.VMEM((MOE_TM * ROW_TILES, LANES), F32), pltpu.SemaphoreType.DMA(()),
                            pltpu.SemaphoreType.DMA(())]),
        out_shape=jax.ShapeDtypeStruct((n_tiles * MOE_TM * ROW_TILES, LANES), F32),
        compiler_params=_cparams(("arbitrary",)),
        name="moe_dispatch",
    )(dest_row, tile_rows, h_lin_p, h_lin_s)


def _moe_kernel(te_ref, tr_ref, nu_ref, first_ref, slot_ref, next_ref, x_ref, wg_hbm, wu_hbm, wd_hbm, y_ref,
                wg_f, wu_f, wd_f, wg_s, wu_s, wd_s, sem):
    i = pl.program_id(0)
    n = tr_ref[i]

    def weight_copies(e, slot):
        return [pltpu.make_async_copy(src.at[e], dst.at[slot], sem.at[slot])
                for src, dst in ((wg_hbm, wg_f), (wu_hbm, wu_f), (wd_hbm, wd_f))]

    @pl.when(i == 0)
    def _():
        for cp in weight_copies(te_ref[0], 0):
            cp.start()

    @pl.when(n == 0)
    def _():
        y_ref[...] = jnp.zeros(y_ref.shape, F32)

    @pl.when(n > 0)
    def _():
        @pl.when(first_ref[i] == 1)
        def _():
            slot = slot_ref[i]
            for cp in weight_copies(te_ref[i], slot):
                cp.wait()
            wg_s[...] = wg_f[slot].astype(BF16)
            wu_s[...] = wu_f[slot].astype(BF16)
            wd_s[...] = wd_f[slot].astype(BF16)

            @pl.when(next_ref[i] >= 0)
            def _():
                for cp in weight_copies(next_ref[i], 1 - slot):
                    cp.start()

        x = _rows_to_tile(x_ref, 0, MOE_TM, ROW_TILES).astype(BF16)
        hg = jnp.dot(x, wg_s[...], preferred_element_type=F32)
        hu = jnp.dot(x, wu_s[...], preferred_element_type=F32)
        act = ((hg * jax.nn.sigmoid(hg)) * hu).astype(BF16)
        y = jnp.dot(act, wd_s[...], preferred_element_type=F32)
        _store_row_linear(y_ref, y, 0, ROW_TILES)


def _moe(x_sorted, plan, w_gate, w_up, w_down, n_tiles):
    rows = MOE_TM * ROW_TILES
    any_spec = pl.BlockSpec(memory_space=pl.ANY)
    return pl.pallas_call(
        _moe_kernel,
        grid_spec=pltpu.PrefetchScalarGridSpec(
            num_scalar_prefetch=len(plan),
            grid=(n_tiles,),
            in_specs=[pl.BlockSpec((rows, LANES), lambda i, te, tr, nu, *_: (jnp.minimum(i, nu[0] - 1), 0)),
                      any_spec, any_spec, any_spec],
            out_specs=pl.BlockSpec((rows, LANES), lambda i, *_: (i, 0)),
            scratch_shapes=[pltpu.VMEM((2, D_MODEL, D_FF), F32), pltpu.VMEM((2, D_MODEL, D_FF), F32),
                            pltpu.VMEM((2, D_FF, D_MODEL), F32),
                            pltpu.VMEM((D_MODEL, D_FF), BF16), pltpu.VMEM((D_MODEL, D_FF), BF16),
                            pltpu.VMEM((D_FF, D_MODEL), BF16), pltpu.SemaphoreType.DMA((2,))]),
        out_shape=jax.ShapeDtypeStruct((n_tiles * rows, LANES), F32),
        compiler_params=_cparams(("arbitrary",)),
        name="moe_experts",
    )(*plan, x_sorted, w_gate, w_up, w_down)


def _combine_kernel(dr_ref, ys_hbm, x_ref, wt_ref, o_ref, g_s, sem, *, tm):
    i = pl.program_id(0)
    nt = pl.num_programs(0)
    slot = i % 2
    unroll = 8

    def issue(tile, dst_slot):
        def body(b, c):
            for u in range(unroll):
                r = b * unroll + u
                for k in range(TOP_K):
                    row = dr_ref[(tile * tm + r) * TOP_K + k]
                    pltpu.make_async_copy(ys_hbm.at[pl.ds(row * ROW_TILES, ROW_TILES), :],
                                          g_s.at[dst_slot, pl.ds((k * tm + r) * ROW_TILES, ROW_TILES), :],
                                          sem.at[dst_slot]).start(priority=k)
            return c
        lax.fori_loop(0, tm // unroll, body, 0)

    @pl.when(i == 0)
    def _():
        issue(0, 0)

    @pl.when(i + 1 < nt)
    def _():
        issue(i + 1, 1 - slot)

    pltpu.make_async_copy(ys_hbm.at[pl.ds(0, TOP_K * tm * ROW_TILES), :], g_s.at[slot], sem.at[slot]).wait()
    wt = wt_ref[...]
    y0 = _rows_to_tile(g_s.at[slot], 0, tm, ROW_TILES)
    y1 = _rows_to_tile(g_s.at[slot], tm * ROW_TILES, tm, ROW_TILES)
    o_ref[...] = x_ref[...] + (wt[:, 0:1] * y0 + wt[:, 1:2] * y1)


def _combine(dest_row, y_sorted, x1, wt, tm):
    t = x1.shape[0]
    tile = lambda i, dr: (i, 0)
    return pl.pallas_call(
        functools.partial(_combine_kernel, tm=tm),
        grid_spec=pltpu.PrefetchScalarGridSpec(
            num_scalar_prefetch=1,
            grid=(t // tm,),
            in_specs=[pl.BlockSpec(memory_space=pl.ANY), pl.BlockSpec((tm, D_MODEL), tile),
                      pl.BlockSpec((tm, LANES), tile)],
            out_specs=pl.BlockSpec((tm, D_MODEL), tile),
            scratch_shapes=[pltpu.VMEM((2, TOP_K * tm * ROW_TILES, LANES), F32), pltpu.SemaphoreType.DMA((2,))]),
        out_shape=jax.ShapeDtypeStruct((t, D_MODEL), F32),
        compiler_params=_cparams(("arbitrary",)),
        name="moe_combine",
    )(dest_row, y_sorted, x1, wt)


def _rotary_tables(pos):
    inv_freq = ROPE_THETA ** (-jnp.arange(0, ROT_DIM, 2, dtype=F32) / ROT_DIM)
    ang = pos[:, None] * inv_freq[None, :]
    cos, sin = jnp.cos(ang), jnp.sin(ang)
    rest = LANES - ROT_DIM
    ones = jnp.ones((pos.shape[0], rest), F32)
    zeros = jnp.zeros((pos.shape[0], rest), F32)
    return jnp.concatenate([cos, cos, ones], axis=1), jnp.concatenate([-sin, sin, zeros], axis=1)


def _projections(x, pos, norm1_g, w_in, q_norm_g, k_norm_g):
    t = x.shape[0]
    tm = min(TM_MAX, t)
    h = _rmsnorm(x, norm1_g, tm)
    tcos, tsin = _rotary_tables(pos)
    c_q = 2 * D_CONV
    c_k = c_q + QK_WIDTH
    c_v = c_k + QK_WIDTH
    c_g = c_v + V_WIDTH
    group = jnp.arange(TN) // HEAD_DIM
    group_ones = (group[:, None] == group[None, :]).astype(BF16)
    rot_in = lambda g: [g.reshape(1, LANES), tcos, tsin, group_ones]
    rot_specs = lambda order: [pl.BlockSpec((1, LANES), lambda a, b: (0, 0)),
                               pl.BlockSpec((tm, LANES), lambda a, b: ((a, b)[order], 0)),
                               pl.BlockSpec((tm, LANES), lambda a, b: ((a, b)[order], 0)),
                               pl.BlockSpec((TN, TN), lambda a, b: (0, 0))]
    u_lin = _glu_call(h, w_in, tm)
    q_b = _stationary_call(_q_kernel, h, w_in, c_q, QK_WIDTH, rot_in(q_norm_g), rot_specs(1), BF16, tm, TN, "proj_q")
    k_lin, k_b = _row_linear_call(_k_kernel, h, w_in, c_k, QK_WIDTH, rot_in(k_norm_g), rot_specs(0), tm, "proj_k")
    v_lin, v_b = _row_linear_call(_v_kernel, h, w_in, c_v, V_WIDTH, [], [], tm, "proj_v")
    gates = _stationary_call(_gate_kernel, h, w_in, c_g, 2 * D_MODEL, [], [], BF16, tm, TN_WIDE, "proj_gates")
    return u_lin, q_b, k_lin, k_b, v_lin, v_b, gates


def _layer(x_p, x_s, cache_k, cache_v, state_conv, page_table, lambda_init,
           norm1_g, w_in, conv_w, conv_b, conv_ln_g, conv_ln_b, w_conv_out, q_norm_g, k_norm_g,
           lambda_q1, lambda_k1, lambda_q2, lambda_k2, subln_g, w_attn_out, w_out, norm2_g,
           w_router_group, b_router_group, w_router_expert, b_router_expert, w_gate_e, w_up_e, w_down_e):
    batch, seq, _ = x_p.shape
    nb, n_new, _ = x_s.shape
    past = page_table.shape[1] * PAGE
    tp, ts = batch * seq, nb * n_new
    xp, xs = x_p.reshape(tp, D_MODEL), x_s.reshape(ts, D_MODEL)
    pos_p = jnp.tile(jnp.arange(seq, dtype=F32), batch)
    pos_s = jnp.tile(past + jnp.arange(n_new, dtype=F32), nb)
    u_p, q_p, kl_p, kb_p, vl_p, vb_p, gates_p = _projections(xp, pos_p, norm1_g, w_in, q_norm_g, k_norm_g)
    u_s, q_s, kl_s, kb_s, vl_s, vb_s, gates_s = _projections(xs, pos_s, norm1_g, w_in, q_norm_g, k_norm_g)

    u_p = u_p.reshape(batch, seq, CONV_TILES, LANES)
    u_s = u_s.reshape(nb, n_new, CONV_TILES, LANES)
    hist_p = jnp.concatenate([jnp.zeros((batch, HIST, CONV_TILES, LANES), F32), u_p], axis=1)
    hist_s = jnp.concatenate([state_conv.reshape(nb, HIST, CONV_TILES, LANES), u_s], axis=1)
    conv = lambda hist, name: _conv_branch(hist, conv_w, conv_b, conv_ln_g, conv_ln_b, name)
    yc_p = conv(hist_p, "conv_prompt").reshape(tp * CONV_TILES, LANES)
    yc_s = conv(hist_s, "conv_sample").reshape(ts * CONV_TILES, LANES)

    lam_rows = jnp.stack([lambda_q1, lambda_k1, lambda_q2, lambda_k2]).astype(F32)
    o_p = _prompt_attention(q_p, kb_p, vb_p, lam_rows, subln_g, batch, seq, lambda_init)
    pad_rows = lambda a: jnp.pad(a.reshape(nb, n_new, -1), ((0, 0), (0, NEW_ROWS - n_new), (0, 0)))
    o_s = _sample_attention(_query_rows(q_s.reshape(nb, n_new, QK_WIDTH)), pad_rows(kb_s), pad_rows(vb_s),
                            cache_k, cache_v, page_table, lam_rows, subln_g, lambda_init, n_new)
    o_s = o_s[:, :n_new].reshape(ts, V_WIDTH).astype(BF16)

    w_r = jnp.zeros((D_MODEL, LANES), F32).at[:, :N_GROUPS].set(w_router_group)
    w_r = w_r.at[:, N_GROUPS:N_GROUPS + N_EXPERTS].set(w_router_expert)
    w_hi = w_r.astype(BF16)
    w_lo = (w_r - w_hi.astype(F32)).astype(BF16)
    w_split = jnp.concatenate([w_hi, w_lo], axis=1)
    b_r = jnp.zeros((1, LANES), F32).at[0, :N_GROUPS].set(b_router_group)
    b_r = b_r.at[0, N_GROUPS:N_GROUPS + N_EXPERTS].set(b_router_expert)

    def tail(x, yc, o_n, gates):
        tm = min(TM_MAX, x.shape[0])
        m = _merge(yc, o_n, w_conv_out, w_attn_out, gates, min(512, tm))
        x1 = _out_projection(m, w_out, x, tm)
        return (x1,) + tuple(_router(x1, norm2_g, w_split, b_r, min(512, x.shape[0])))

    x1_p, hl_p, e_p, wt_p = tail(xp, yc_p, o_p, gates_p)
    x1_s, hl_s, e_s, wt_s = tail(xs, yc_s, o_s, gates_s)

    n_pairs = (tp + ts) * TOP_K
    n_tiles = -(-n_pairs // MOE_TM) + N_EXPERTS
    e_all = jnp.concatenate([e_p[:, :TOP_K], e_s[:, :TOP_K]], axis=0)
    dest_row, moe_plan = _routing_plan(e_all, n_tiles)
    x_sorted = _dispatch(dest_row, moe_plan[1], hl_p, hl_s, n_tiles)
    y_sorted = _moe(x_sorted, moe_plan, w_gate_e, w_up_e, w_down_e, n_tiles)
    y_p = _combine(dest_row[:tp * TOP_K], y_sorted, x1_p, wt_p, CMB_TM)
    y_s = _combine(dest_row[tp * TOP_K:], y_sorted, x1_s, wt_s, min(CMB_TM, ts))

    v_out = lambda a, lead: a.reshape(lead + (2, N_HEADS, LANES)).swapaxes(-3, -2).reshape(lead + (N_HEADS, V_DIM))
    return (y_p.reshape(batch, seq, D_MODEL), y_s.reshape(nb, n_new, D_MODEL),
            kl_p.reshape(batch, seq, N_HEADS, 2, HEAD_DIM), v_out(vl_p, (batch, seq)),
            hist_p[:, seq:].reshape(batch, HIST, D_CONV),
            kl_s.reshape(nb, n_new, N_HEADS, 2, HEAD_DIM), v_out(vl_s, (nb, n_new)),
            hist_s[:, n_new:].reshape(nb, HIST, D_CONV))


def kernel(x_prompt, x_sample, cache_k, cache_v, state_conv, page_table, norm1_g, w_in, conv_w, conv_b, conv_ln_g, conv_ln_b, w_conv_out, q_norm_g, k_norm_g, lambda_q1, lambda_k1, lambda_q2, lambda_k2, subln_g, w_attn_out, w_out, norm2_g, w_router_group, b_router_group, w_router_expert, b_router_expert, w_gate_e, w_up_e, w_down_e):
    depth = norm1_g.shape[0]
    x_p, x_s = x_prompt, x_sample
    outs = [[] for _ in range(6)]
    for l in range(depth):
        lambda_init = 0.8 - 0.6 * math.exp(-0.3 * l)
        x_p, x_s, *rest = _layer(
            x_p, x_s, cache_k[l], cache_v[l], state_conv[l], page_table, lambda_init,
            norm1_g[l], w_in[l], conv_w[l], conv_b[l], conv_ln_g[l], conv_ln_b[l], w_conv_out[l],
            q_norm_g[l], k_norm_g[l], lambda_q1[l], lambda_k1[l], lambda_q2[l], lambda_k2[l], subln_g[l],
            w_attn_out[l], w_out[l], norm2_g[l], w_router_group[l], b_router_group[l],
            w_router_expert[l], b_router_expert[l], w_gate_e[l], w_up_e[l], w_down_e[l])
        for acc, val in zip(outs, rest):
            acc.append(val)
    return (x_p, x_s) + tuple(jnp.stack(o) for o in outs)
```

```python
import functools
import math

import jax
import jax.numpy as jnp
from jax import lax
from jax.experimental import pallas as pl
from jax.experimental.pallas import tpu as pltpu

F32 = jnp.float32
BF16 = jnp.bfloat16

D_MODEL = 2048
N_HEADS = 8
HEAD_DIM = 128
V_DIM = 2 * HEAD_DIM
QK_WIDTH = N_HEADS * 2 * HEAD_DIM
V_WIDTH = N_HEADS * V_DIM
ROT_DIM = HEAD_DIM // 4
ROPE_THETA = 500000.0
ATTN_SCALE = HEAD_DIM ** -0.5
LOG2E = math.log2(math.e)
NEG_INF = -1e30
D_CONV = D_MODEL // 2
CONV_WIDTH = 31
HIST = CONV_WIDTH - 1
N_GROUPS = 4
EXPERTS_PER_GROUP = 8
N_EXPERTS = N_GROUPS * EXPERTS_PER_GROUP
TOP_K = 2
D_FF = D_MODEL // 4
EPS = 1e-6
PAGE = 128

LANES = 128
SUBLANES = 8
ROW_TILES = D_MODEL // LANES
CONV_TILES = D_CONV // LANES
VMEM_LIMIT = 56 * 1024 * 1024

TM_MAX = 1024
TN = 512
TN_WIDE = 1024
ATTN_ROWS = 256
PAGES_PER_STEP = 4
QROWS = 2 * 4 * SUBLANES
NEW_ROWS = 16
MOE_TM = 256
CMB_TM = 256
CONV_ROWS = 32


def _cparams(sem):
    return pltpu.CompilerParams(dimension_semantics=sem, vmem_limit_bytes=VMEM_LIMIT)


def _rows_to_tile(buf, base, rows, tiles):
    return jnp.concatenate([buf[pl.ds(base + c, rows, stride=tiles), :] for c in range(tiles)], axis=1)


def _rmsnorm_kernel(x_ref, g_ref, o_ref):
    x = x_ref[...]
    ms = jnp.mean(x * x, axis=-1, keepdims=True)
    o_ref[...] = ((x * lax.rsqrt(ms + EPS)) * g_ref[...]).astype(o_ref.dtype)


def _rmsnorm(x, g, tm):
    t, d = x.shape
    return pl.pallas_call(
        _rmsnorm_kernel,
        grid=(t // tm,),
        in_specs=[pl.BlockSpec((tm, d), lambda i: (i, 0)), pl.BlockSpec((1, d), lambda i: (0, 0))],
        out_specs=pl.BlockSpec((tm, d), lambda i: (i, 0)),
        out_shape=jax.ShapeDtypeStruct((t, d), BF16),
        compiler_params=_cparams(("arbitrary",)),
        name="rmsnorm1",
    )(x, g.reshape(1, d))


def _cache_weight(w_ref, w_s):
    @pl.when(pl.program_id(1) == 0)
    def _():
        w_s[...] = w_ref[...].astype(BF16)


def _store_row_linear(o_ref, val, first_tile, tiles, row_of_tile=None):
    rows = val.shape[0]
    for c in range(val.shape[1] // LANES):
        tile = first_tile + c
        dst = tile if row_of_tile is None else row_of_tile(tile)
        o_ref[pl.ds(dst, rows, stride=tiles), :] = val[:, c * LANES:(c + 1) * LANES]


def _glu_kernel(h_ref, wa_ref, wb_ref, u_ref):
    h = h_ref[...]
    a = jnp.dot(h, wa_ref[...].astype(BF16), preferred_element_type=F32)
    b = jnp.dot(h, wb_ref[...].astype(BF16), preferred_element_type=F32)
    _store_row_linear(u_ref, a * jax.nn.sigmoid(b), pl.program_id(1) * (TN // LANES), CONV_TILES)


def _glu_call(h, w, tm):
    t, k = h.shape
    nj = D_CONV // TN
    return pl.pallas_call(
        _glu_kernel,
        grid=(t // tm, nj),
        in_specs=[pl.BlockSpec((tm, k), lambda i, j: (i, 0)), pl.BlockSpec((k, TN), lambda i, j: (0, j)),
                  pl.BlockSpec((k, TN), lambda i, j: (0, nj + j))],
        out_specs=pl.BlockSpec((tm * CONV_TILES, LANES), lambda i, j: (i, 0)),
        out_shape=jax.ShapeDtypeStruct((t * CONV_TILES, LANES), F32),
        compiler_params=_cparams(("arbitrary", "arbitrary")),
        name="proj_glu",
    )(h, w, w)


def _head_norm_rotary(acc, ones_ref, gain, tcos, tsin):
    ssq = jnp.dot((acc * acc).astype(BF16), ones_ref[...], preferred_element_type=F32)
    tn = acc * lax.rsqrt(ssq * (1.0 / HEAD_DIM) + EPS)
    lane = lax.broadcasted_iota(jnp.int32, (acc.shape[0], LANES), 1)
    half = ROT_DIM // 2
    outs = []
    for g in range(acc.shape[1] // LANES):
        t = tn[:, g * LANES:(g + 1) * LANES] * gain
        partner = jnp.where(lane < half, pltpu.roll(t, LANES - half, 1), pltpu.roll(t, half, 1))
        outs.append(t * tcos + partner * tsin)
    return jnp.concatenate(outs, axis=1)


def _q_kernel(h_ref, w_ref, g_ref, tc_ref, ts_ref, ones_ref, ob_ref, w_s):
    _cache_weight(w_ref, w_s)
    acc = jnp.dot(h_ref[...], w_s[...], preferred_element_type=F32)
    y = _head_norm_rotary(acc, ones_ref, g_ref[...], tc_ref[...], ts_ref[...])
    ob_ref[...] = (y * (ATTN_SCALE * LOG2E)).astype(BF16)


def _gate_kernel(h_ref, w_ref, o_ref, w_s):
    _cache_weight(w_ref, w_s)
    o_ref[...] = jax.nn.sigmoid(jnp.dot(h_ref[...], w_s[...], preferred_element_type=F32)).astype(o_ref.dtype)


def _stationary_call(kernel, h, w, col0, width, extra_in, extra_specs, out_dtype, tm, tn, name):
    t, k = h.shape
    return pl.pallas_call(
        kernel,
        grid=(width // tn, t // tm),
        in_specs=[pl.BlockSpec((tm, k), lambda j, i: (i, 0)),
                  pl.BlockSpec((k, tn), lambda j, i: (0, col0 // tn + j))] + extra_specs,
        out_specs=pl.BlockSpec((tm, tn), lambda j, i: (i, j)),
        out_shape=jax.ShapeDtypeStruct((t, width), out_dtype),
        scratch_shapes=[pltpu.VMEM((k, tn), BF16)],
        compiler_params=_cparams(("arbitrary", "arbitrary")),
        name=name,
    )(h, w, *extra_in)


def _k_kernel(h_ref, w_ref, g_ref, tc_ref, ts_ref, ones_ref, of_ref, ob_ref):
    acc = jnp.dot(h_ref[...], w_ref[...].astype(BF16), preferred_element_type=F32)
    y = _head_norm_rotary(acc, ones_ref, g_ref[...], tc_ref[...], ts_ref[...])
    _store_row_linear(of_ref, y, pl.program_id(1) * (TN // LANES), ROW_TILES)
    ob_ref[...] = y.astype(BF16)


def _v_kernel(h_ref, w_ref, of_ref, ob_ref):
    acc = jnp.dot(h_ref[...], w_ref[...].astype(BF16), preferred_element_type=F32)
    _store_row_linear(of_ref, acc, pl.program_id(1) * (TN // LANES), ROW_TILES,
                      row_of_tile=lambda tile: (tile % 2) * N_HEADS + tile // 2)
    ob_ref[...] = acc.astype(BF16)


def _row_linear_call(kernel, h, w, col0, width, extra_in, extra_specs, tm, name):
    t, k = h.shape
    tiles = width // LANES
    return pl.pallas_call(
        kernel,
        grid=(t // tm, width // TN),
        in_specs=[pl.BlockSpec((tm, k), lambda i, j: (i, 0)),
                  pl.BlockSpec((k, TN), lambda i, j: (0, col0 // TN + j))] + extra_specs,
        out_specs=[pl.BlockSpec((tm * tiles, LANES), lambda i, j: (i, 0)), pl.BlockSpec((tm, TN), lambda i, j: (i, j))],
        out_shape=[jax.ShapeDtypeStruct((t * tiles, LANES), F32), jax.ShapeDtypeStruct((t, width), BF16)],
        compiler_params=_cparams(("arbitrary", "arbitrary")),
        name=name,
    )(h, w, *extra_in)


def _conv_kernel(hist_ref, w_ref, b_ref, g_ref, beta_ref, o_ref, *, length):
    rows = min(CONV_ROWS, length)

    def step(i, carry):
        t0 = i * rows
        acc = jnp.zeros((rows, CONV_TILES, LANES), F32)
        for j in range(CONV_WIDTH):
            acc = acc + hist_ref[0, pl.ds(t0 + j, rows)] * w_ref[j]
        y = acc + b_ref[...]
        mu = jnp.mean(y, axis=(1, 2), keepdims=True)
        yc = y - mu
        var = jnp.mean(yc * yc, axis=(1, 2), keepdims=True)
        z = yc * lax.rsqrt(var + EPS) * g_ref[...] + beta_ref[...]
        o_ref[0, pl.ds(t0, rows)] = z * jax.nn.sigmoid(z)
        return carry

    trips = length // rows
    lax.fori_loop(0, trips, step, 0, unroll=2 if trips % 2 == 0 else 1)


def _conv_branch(hist, conv_w, conv_b, ln_g, ln_b, name):
    b, lh = hist.shape[:2]
    length = lh - HIST
    tile = lambda a: a.reshape(-1, CONV_TILES, LANES)
    whole = lambda n: pl.BlockSpec((n, CONV_TILES, LANES), lambda i: (0, 0, 0))
    return pl.pallas_call(
        functools.partial(_conv_kernel, length=length),
        grid=(b,),
        in_specs=[pl.BlockSpec((1, lh, CONV_TILES, LANES), lambda i: (i, 0, 0, 0)),
                  whole(CONV_WIDTH), whole(1), whole(1), whole(1)],
        out_specs=pl.BlockSpec((1, length, CONV_TILES, LANES), lambda i: (i, 0, 0, 0)),
        out_shape=jax.ShapeDtypeStruct((b, length, CONV_TILES, LANES), F32),
        compiler_params=_cparams(("arbitrary",)),
        name=name,
    )(hist, tile(conv_w), tile(conv_b), tile(ln_g), tile(ln_b))


def _lambda_value(lam_ref, lambda_init):
    v = lam_ref[...]
    s1 = jnp.sum(v[0:1] * v[1:2], axis=-1, keepdims=True)
    s2 = jnp.sum(v[2:3] * v[3:4], axis=-1, keepdims=True)
    return jnp.exp(s1) - jnp.exp(s2) + lambda_init


def _prompt_attn_kernel(q_ref, k_ref, v_ref, lam_ref, sg_ref, o_ref, *, lambda_init, seq):
    rb = ATTN_ROWS
    lam = _lambda_value(lam_ref, lambda_init)
    row = lax.broadcasted_iota(jnp.int32, (rb, rb), 0)
    col = lax.broadcasted_iota(jnp.int32, (rb, rb), 1)
    for r in range(seq // rb):
        rows = pl.ds(r * rb, rb)
        kc = (r + 1) * rb
        v = v_ref[0:kc, :]
        outs = []
        for c in range(2):
            q = q_ref[rows, c * HEAD_DIM:(c + 1) * HEAD_DIM]
            k = k_ref[0:kc, c * HEAD_DIM:(c + 1) * HEAD_DIM]
            s = lax.dot_general(q, k, (((1,), (1,)), ((), ())), preferred_element_type=F32)
            diag = jnp.where(col <= row, s[:, kc - rb:], NEG_INF)
            s = diag if r == 0 else jnp.concatenate([s[:, :kc - rb], diag], axis=1)
            p = jnp.exp2(s - jnp.max(s, axis=-1, keepdims=True))
            l = jnp.sum(p, axis=-1, keepdims=True)
            outs.append(jnp.dot(p.astype(BF16), v, preferred_element_type=F32) / l)
        o = outs[0] - lam * outs[1]
        ms = jnp.mean(o * o, axis=-1, keepdims=True)
        o_ref[rows, :] = (((o * lax.rsqrt(ms + EPS)) * sg_ref[...]) * (1.0 - lambda_init)).astype(o_ref.dtype)


def _prompt_attention(q, k, v, lam_rows, subln_g, batch, seq, lambda_init):
    blk = lambda b, h: (b, h)
    return pl.pallas_call(
        functools.partial(_prompt_attn_kernel, lambda_init=lambda_init, seq=seq),
        grid=(batch, N_HEADS),
        in_specs=[pl.BlockSpec((seq, V_DIM), blk), pl.BlockSpec((seq, V_DIM), blk), pl.BlockSpec((seq, V_DIM), blk),
                  pl.BlockSpec((4, HEAD_DIM), lambda b, h: (0, 0)), pl.BlockSpec((1, V_DIM), lambda b, h: (0, 0))],
        out_specs=pl.BlockSpec((seq, V_DIM), blk),
        out_shape=jax.ShapeDtypeStruct((batch * seq, V_WIDTH), BF16),
        compiler_params=_cparams(("arbitrary", "arbitrary")),
        name="prompt_attention",
    )(q, k, v, lam_rows, subln_g.reshape(1, V_DIM))


def _page_tiles(x):
    tiles = jnp.swapaxes(x, 0, 1)
    return [tiles[i].astype(BF16) for i in range(SUBLANES)]


def _sample_attn_kernel(pt_ref, *refs, n_new, lambda_init):
    g = PAGES_PER_STEP
    k_refs, v_refs = refs[:2 * g], refs[2 * g:4 * g]
    q_ref, kn_ref, vn_ref, lam_ref, sg_ref, o_ref, m_s, l_s, acc_s = refs[4 * g:]
    j = pl.program_id(1)
    q2 = q_ref[0]
    nt_dims = (((1,), (1,)), ((), ()))

    @pl.when(j == 0)
    def _():
        m_s[...] = jnp.full(m_s.shape, NEG_INF, F32)
        l_s[...] = jnp.zeros(l_s.shape, F32)
        acc_s[...] = jnp.zeros(acc_s.shape, F32)

    def online_update(s, values):
        m_old = m_s[...]
        m_new = jnp.maximum(m_old, jnp.max(s, axis=-1, keepdims=True))
        alpha = jnp.exp2(m_old - m_new)
        p = jnp.exp2(s - m_new)
        l_s[...] = alpha * l_s[...] + jnp.sum(p, axis=-1, keepdims=True)
        pv = None
        c0 = 0
        for n, v in values:
            part = jnp.dot(p[:, c0:c0 + n].astype(BF16), v, preferred_element_type=F32)
            pv = part if pv is None else pv + part
            c0 += n
        acc_s[...] = alpha * acc_s[...] + pv
        m_s[...] = m_new

    s_parts, values = [], []
    for p in range(g):
        kfull = jnp.concatenate(_page_tiles(k_refs[2 * p][...]) + _page_tiles(k_refs[2 * p + 1][...]), axis=1)
        s_parts.append(lax.dot_general(q2, kfull, nt_dims, preferred_element_type=F32))
        t0, t1 = _page_tiles(v_refs[2 * p][...]), _page_tiles(v_refs[2 * p + 1][...])
        values.append((PAGE, jnp.concatenate([t for pair in zip(t0, t1) for t in pair], axis=1)))
    online_update(jnp.concatenate(s_parts, axis=1), values)

    @pl.when(j == pl.num_programs(1) - 1)
    def _():
        kn = kn_ref[0]
        s = lax.dot_general(q2, kn, nt_dims, preferred_element_type=F32)
        key = lax.broadcasted_iota(jnp.int32, s.shape, 1)
        qry = (lax.broadcasted_iota(jnp.int32, s.shape, 0) % (4 * SUBLANES)) // SUBLANES
        s = jnp.where((key <= qry) & (key < n_new), s, NEG_INF)
        online_update(s, [(kn.shape[0], vn_ref[0])])
        lam = _lambda_value(lam_ref, lambda_init)
        half = QROWS // 2
        acc, l = acc_s[...], l_s[...]
        o_all = acc[:half] / l[:half] - lam * (acc[half:] / l[half:])
        head = lax.broadcasted_iota(jnp.int32, (SUBLANES, V_WIDTH), 0)
        own = head == lax.broadcasted_iota(jnp.int32, (SUBLANES, V_WIDTH), 1) // V_DIM
        rows = []
        for q in range(n_new):
            o = jnp.where(own, o_all[q * SUBLANES:(q + 1) * SUBLANES, :], 0.0)
            ms = jnp.sum(o * o, axis=-1, keepdims=True) / V_DIM
            o = jnp.sum(o * lax.rsqrt(ms + EPS), axis=0, keepdims=True)
            rows.append((o * sg_ref[...]) * (1.0 - lambda_init))
        rows.append(jnp.zeros((SUBLANES - n_new, V_WIDTH), F32))
        o_ref[0] = jnp.concatenate(rows, axis=0)


def _sample_attention(q_rows, k_new, v_new, cache_k, cache_v, page_table, lam_rows, subln_g, lambda_init, n_new):
    nb, n_pages = page_table.shape
    g = PAGES_PER_STEP
    n_pool = cache_k.shape[0]
    ck = cache_k.reshape(n_pool * PAGE, 2 * SUBLANES, HEAD_DIM)
    cv = cache_v.reshape(n_pool * PAGE, N_HEADS, V_DIM)
    pt = page_table.reshape(-1)
    kmap = lambda p, a: (lambda b, j, pt: (pt[b * n_pages + j * g + p], a, 0))
    vmap = lambda p, a: (lambda b, j, pt: (pt[b * n_pages + j * g + p], 0, a))
    k_specs = [pl.BlockSpec((PAGE, SUBLANES, HEAD_DIM), kmap(p, a)) for p in range(g) for a in range(2)]
    v_specs = [pl.BlockSpec((PAGE, N_HEADS, LANES), vmap(p, a)) for p in range(g) for a in range(2)]
    new_rows = k_new.shape[1]
    per_b = lambda shape: pl.BlockSpec((1,) + shape, lambda b, j, pt: (b, 0, 0))
    const = lambda shape: pl.BlockSpec(shape, lambda b, j, pt: (0, 0))
    tiled_g = jnp.tile(subln_g.reshape(1, V_DIM), (1, N_HEADS))
    return pl.pallas_call(
        functools.partial(_sample_attn_kernel, n_new=n_new, lambda_init=lambda_init),
        grid_spec=pltpu.PrefetchScalarGridSpec(
            num_scalar_prefetch=1,
            grid=(nb, n_pages // g),
            in_specs=k_specs + v_specs + [per_b((QROWS, QK_WIDTH)), per_b((new_rows, QK_WIDTH)),
                                          per_b((new_rows, V_WIDTH)), const((4, HEAD_DIM)), const((1, V_WIDTH))],
            out_specs=per_b((SUBLANES, V_WIDTH)),
            scratch_shapes=[pltpu.VMEM((QROWS, 1), F32), pltpu.VMEM((QROWS, 1), F32), pltpu.VMEM((QROWS, V_WIDTH), F32)]),
        out_shape=jax.ShapeDtypeStruct((nb, SUBLANES, V_WIDTH), F32),
        compiler_params=_cparams(("arbitrary", "arbitrary")),
        name="sample_attention",
    )(pt, *([ck] * (2 * g)), *([cv] * (2 * g)), q_rows, k_new, v_new, lam_rows, tiled_g)


def _query_rows(q_s):
    n_new = q_s.shape[1]
    r = jnp.arange(QROWS)
    r_c, r_q, r_h = r // (4 * SUBLANES), (r % (4 * SUBLANES)) // SUBLANES, r % SUBLANES
    col_hc = jnp.arange(QK_WIDTH) // HEAD_DIM
    keep = (col_hc[None, :] == (r_h * 2 + r_c)[:, None]) & (r_q < n_new)[:, None]
    picked = q_s[:, jnp.minimum(r_q, n_new - 1), :]
    return jnp.where(keep[None], picked, jnp.zeros((), q_s.dtype))


def _merge_kernel(yc_ref, ya_ref, wc_ref, wa_ref, gc_ref, ga_ref, m_ref, wc_s, wa_s):
    _cache_weight(wc_ref, wc_s)
    _cache_weight(wa_ref, wa_s)
    conv = _rows_to_tile(yc_ref, 0, ya_ref.shape[0], CONV_TILES).astype(BF16)
    yc = jnp.dot(conv, wc_s[...], preferred_element_type=F32)
    ya = jnp.dot(ya_ref[...], wa_s[...], preferred_element_type=F32)
    m_ref[...] = (gc_ref[...] * yc + ga_ref[...] * ya).astype(m_ref.dtype)


def _merge(yconv_lin, o_n, w_conv_out, w_attn_out, gates, tm):
    t = o_n.shape[0]
    tn = TN_WIDE
    nj, ni = D_MODEL // tn, t // tm
    return pl.pallas_call(
        _merge_kernel,
        grid=(nj, ni),
        in_specs=[pl.BlockSpec((tm * CONV_TILES, LANES), lambda j, i: (i, 0)),
                  pl.BlockSpec((tm, V_WIDTH), lambda j, i: (i, 0)),
                  pl.BlockSpec((D_CONV, tn), lambda j, i: (0, j)), pl.BlockSpec((V_WIDTH, tn), lambda j, i: (0, j)),
                  pl.BlockSpec((tm, tn), lambda j, i: (i, j)), pl.BlockSpec((tm, tn), lambda j, i: (i, nj + j))],
        out_specs=pl.BlockSpec((tm, tn), lambda j, i: (i, j)),
        out_shape=jax.ShapeDtypeStruct((t, D_MODEL), BF16),
        scratch_shapes=[pltpu.VMEM((D_CONV, tn), BF16), pltpu.VMEM((V_WIDTH, tn), BF16)],
        compiler_params=_cparams(("arbitrary", "arbitrary")),
        name="merge",
    )(yconv_lin, o_n, w_conv_out, w_attn_out, gates, gates)


def _residual_kernel(m_ref, w_ref, x_ref, o_ref, w_s):
    _cache_weight(w_ref, w_s)
    o_ref[...] = x_ref[...] + jnp.dot(m_ref[...], w_s[...], preferred_element_type=F32)


def _out_projection(m, w_out, x, tm):
    t = m.shape[0]
    tn = TN_WIDE
    return pl.pallas_call(
        _residual_kernel,
        grid=(D_MODEL // tn, t // tm),
        in_specs=[pl.BlockSpec((tm, D_MODEL), lambda j, i: (i, 0)), pl.BlockSpec((D_MODEL, tn), lambda j, i: (0, j)),
                  pl.BlockSpec((tm, tn), lambda j, i: (i, j))],
        out_specs=pl.BlockSpec((tm, tn), lambda j, i: (i, j)),
        out_shape=jax.ShapeDtypeStruct((t, D_MODEL), F32),
        scratch_shapes=[pltpu.VMEM((D_MODEL, tn), BF16)],
        compiler_params=_cparams(("arbitrary", "arbitrary")),
        name="out_projection",
    )(m, w_out, x)


def _router_kernel(x_ref, g_ref, w_ref, b_ref, hl_ref, e_ref, wt_ref):
    x = x_ref[...]
    ms = jnp.mean(x * x, axis=-1, keepdims=True)
    h = (x * lax.rsqrt(ms + EPS)) * g_ref[...]
    _store_row_linear(hl_ref, h, 0, ROW_TILES)
    h_hi = h.astype(BF16)
    h_lo = (h - h_hi.astype(F32)).astype(BF16)
    both = jnp.dot(h_hi, w_ref[...], preferred_element_type=F32)
    logits = both[:, :LANES] + both[:, LANES:] + jnp.dot(h_lo, w_ref[:, :LANES], preferred_element_type=F32)
    logits = logits + b_ref[...]
    lane = lax.broadcasted_iota(jnp.int32, logits.shape, 1)
    neg = -jnp.inf
    big = jnp.int32(LANES)
    lg = jnp.where(lane < N_GROUPS, logits, neg)
    eg = jnp.exp(lg - jnp.max(lg, axis=-1, keepdims=True))
    pg = eg / jnp.sum(eg, axis=-1, keepdims=True)
    p_top = jnp.max(pg, axis=-1, keepdims=True)
    g_idx = jnp.min(jnp.where(pg == p_top, lane, big), axis=-1, keepdims=True)
    in_group = (lane >= N_GROUPS) & (lane < N_GROUPS + N_EXPERTS) & (((lane - N_GROUPS) // EXPERTS_PER_GROUP) == g_idx)
    le = jnp.where(in_group, logits, neg)
    ee = jnp.exp(le - jnp.max(le, axis=-1, keepdims=True))
    pe = jnp.where(in_group, ee / jnp.sum(ee, axis=-1, keepdims=True), -1.0)
    w1 = jnp.max(pe, axis=-1, keepdims=True)
    i1 = jnp.min(jnp.where(pe == w1, lane, big), axis=-1, keepdims=True)
    pe2 = jnp.where(lane == i1, -1.0, pe)
    w2 = jnp.max(pe2, axis=-1, keepdims=True)
    i2 = jnp.min(jnp.where(pe2 == w2, lane, big), axis=-1, keepdims=True)
    denom = w1 + w2
    e_ref[...] = jnp.where(lane == 0, i1 - N_GROUPS, jnp.where(lane == 1, i2 - N_GROUPS, 0))
    wt_ref[...] = jnp.where(lane == 0, w1 / denom * p_top, jnp.where(lane == 1, w2 / denom * p_top, 0.0))


def _router(x1, norm2_g, w_router, b_router, tm):
    t = x1.shape[0]
    return pl.pallas_call(
        _router_kernel,
        grid=(t // tm,),
        in_specs=[pl.BlockSpec((tm, D_MODEL), lambda i: (i, 0)), pl.BlockSpec((1, D_MODEL), lambda i: (0, 0)),
                  pl.BlockSpec((D_MODEL, 2 * LANES), lambda i: (0, 0)), pl.BlockSpec((1, LANES), lambda i: (0, 0))],
        out_specs=[pl.BlockSpec((tm * ROW_TILES, LANES), lambda i: (i, 0)),
                   pl.BlockSpec((tm, LANES), lambda i: (i, 0)), pl.BlockSpec((tm, LANES), lambda i: (i, 0))],
        out_shape=[jax.ShapeDtypeStruct((t * ROW_TILES, LANES), F32),
                   jax.ShapeDtypeStruct((t, LANES), jnp.int32), jax.ShapeDtypeStruct((t, LANES), F32)],
        compiler_params=_cparams(("arbitrary",)),
        name="router",
    )(x1, norm2_g.reshape(1, D_MODEL), w_router, b_router)


def _routing_plan(expert_idx, n_tiles):
    e_flat = expert_idx.reshape(-1)
    onehot = (e_flat[:, None] == jnp.arange(N_EXPERTS, dtype=jnp.int32)[None, :]).astype(jnp.int32)
    csum = jnp.cumsum(onehot, axis=0)
    counts = csum[-1]
    rank = jnp.sum(onehot * csum, axis=1) - 1
    tiles_e = (counts + MOE_TM - 1) // MOE_TM
    tile_end = jnp.cumsum(tiles_e)
    tile_start = tile_end - tiles_e
    dest_row = jnp.sum(onehot * tile_start[None, :], axis=1) * MOE_TM + rank
    n_used = tile_end[-1]
    tile_id = jnp.arange(n_tiles, dtype=jnp.int32)
    te = jnp.sum((tile_end[None, :] <= jnp.minimum(tile_id, n_used - 1)[:, None]).astype(jnp.int32), axis=1)
    te = jnp.minimum(te, N_EXPERTS - 1)
    left = counts[te] - (tile_id - tile_start[te]) * MOE_TM
    tile_rows = jnp.where(tile_id < n_used, jnp.clip(left, 0, MOE_TM), 0)
    first = (tile_id < n_used) & (tile_id == tile_start[te])
    slot = (jnp.cumsum(first.astype(jnp.int32)) - 1) % 2
    ids = jnp.arange(N_EXPERTS, dtype=jnp.int32)
    later_used = (ids[None, :] > ids[:, None]) & (counts > 0)[None, :]
    next_used = jnp.min(jnp.where(later_used, ids[None, :], N_EXPERTS), axis=1)
    next_expert = jnp.where(next_used[te] < N_EXPERTS, next_used[te], -1)
    i32 = lambda a: a.astype(jnp.int32)
    moe_plan = (i32(te), i32(tile_rows), i32(n_used).reshape(1), i32(first), i32(slot), i32(next_expert))
    return i32(dest_row), moe_plan


def _dispatch_kernel(dr_ref, tr_ref, hp_ref, hs_ref, xs_hbm, zero_s, sem, zsem, *, tile_p, p_steps, n_sample, n_tiles):
    i = pl.program_id(0)
    unroll = 8

    def scatter(src_ref, n_tok, pair0):
        def body(b, c):
            for u in range(unroll):
                r = b * unroll + u
                for k in range(TOP_K):
                    dst = dr_ref[pair0 + r * TOP_K + k]
                    pltpu.make_async_copy(src_ref.at[pl.ds(r * ROW_TILES, ROW_TILES), :],
                                          xs_hbm.at[pl.ds(dst * ROW_TILES, ROW_TILES), :], sem).start(priority=k)
            return c

        lax.fori_loop(0, n_tok // unroll, body, 0)
        for _ in range(TOP_K):
            pltpu.make_async_copy(src_ref, xs_hbm.at[pl.ds(0, n_tok * ROW_TILES), :], sem).wait()

    @pl.when(i < p_steps)
    def _():
        scatter(hp_ref, tile_p, i * (tile_p * TOP_K))

    @pl.when(i == p_steps)
    def _():
        scatter(hs_ref, n_sample, p_steps * tile_p * TOP_K)
        zero_s[...] = jnp.zeros(zero_s.shape, F32)

        def pad_body(t, carry):
            n = tr_ref[t]

            @pl.when(n < MOE_TM)
            def _():
                rows = (MOE_TM - n) * ROW_TILES
                cp = pltpu.make_async_copy(zero_s.at[pl.ds(0, rows), :],
                                           xs_hbm.at[pl.ds((t * MOE_TM + n) * ROW_TILES, rows), :], zsem)
                cp.start()
                cp.wait()
            return carry

        lax.fori_loop(0, n_tiles, pad_body, 0)


def _dispatch(dest_row, tile_rows, h_lin_p, h_lin_s, n_tiles):
    n_prompt, n_sample = h_lin_p.shape[0] // ROW_TILES, h_lin_s.shape[0] // ROW_TILES
    tile_p = min(512, n_prompt)
    p_steps = n_prompt // tile_p
    return pl.pallas_call(
        functools.partial(_dispatch_kernel, tile_p=tile_p, p_steps=p_steps, n_sample=n_sample, n_tiles=n_tiles),
        grid_spec=pltpu.PrefetchScalarGridSpec(
            num_scalar_prefetch=2,
            grid=(p_steps + 1,),
            in_specs=[pl.BlockSpec((tile_p * ROW_TILES, LANES), lambda i, dr, tr: (jnp.minimum(i, p_steps - 1), 0)),
                      pl.BlockSpec((n_sample * ROW_TILES, LANES), lambda i, dr, tr: (0, 0))],
            out_specs=pl.BlockSpec(memory_space=pl.ANY),
            scratch_shapes=[pltpu.VMEM((MOE_TM * ROW_TILES, LANES), F32), pltpu.SemaphoreType.DMA(()),
                            pltpu.SemaphoreType.DMA(())]),
        out_shape=jax.ShapeDtypeStruct((n_tiles * MOE_TM * ROW_TILES, LANES), F32),
        compiler_params=_cparams(("arbitrary",)),
        name="moe_dispatch",
    )(dest_row, tile_rows, h_lin_p, h_lin_s)


def _moe_kernel(te_ref, tr_ref, nu_ref, first_ref, slot_ref, next_ref, x_ref, wg_hbm, wu_hbm, wd_hbm, y_ref,
                wg_f, wu_f, wd_f, wg_s, wu_s, wd_s, sem):
    i = pl.program_id(0)
    n = tr_ref[i]

    def weight_copies(e, slot):
        return [pltpu.make_async_copy(src.at[e], dst.at[slot], sem.at[slot])
                for src, dst in ((wg_hbm, wg_f), (wu_hbm, wu_f), (wd_hbm, wd_f))]

    @pl.when(i == 0)
    def _():
        for cp in weight_copies(te_ref[0], 0):
            cp.start()

    @pl.when(n == 0)
    def _():
        y_ref[...] = jnp.zeros(y_ref.shape, F32)

    @pl.when(n > 0)
    def _():
        @pl.when(first_ref[i] == 1)
        def _():
            slot = slot_ref[i]
            for cp in weight_copies(te_ref[i], slot):
                cp.wait()
            wg_s[...] = wg_f[slot].astype(BF16)
            wu_s[...] = wu_f[slot].astype(BF16)
            wd_s[...] = wd_f[slot].astype(BF16)

            @pl.when(next_ref[i] >= 0)
            def _():
                for cp in weight_copies(next_ref[i], 1 - slot):
                    cp.start()

        x = _rows_to_tile(x_ref, 0, MOE_TM, ROW_TILES).astype(BF16)
        hg = jnp.dot(x, wg_s[...], preferred_element_type=F32)
        hu = jnp.dot(x, wu_s[...], preferred_element_type=F32)
        act = ((hg * jax.nn.sigmoid(hg)) * hu).astype(BF16)
        y = jnp.dot(act, wd_s[...], preferred_element_type=F32)
        _store_row_linear(y_ref, y, 0, ROW_TILES)


def _moe(x_sorted, plan, w_gate, w_up, w_down, n_tiles):
    rows = MOE_TM * ROW_TILES
    any_spec = pl.BlockSpec(memory_space=pl.ANY)
    return pl.pallas_call(
        _moe_kernel,
        grid_spec=pltpu.PrefetchScalarGridSpec(
            num_scalar_prefetch=len(plan),
            grid=(n_tiles,),
            in_specs=[pl.BlockSpec((rows, LANES), lambda i, te, tr, nu, *_: (jnp.minimum(i, nu[0] - 1), 0)),
                      any_spec, any_spec, any_spec],
            out_specs=pl.BlockSpec((rows, LANES), lambda i, *_: (i, 0)),
            scratch_shapes=[pltpu.VMEM((2, D_MODEL, D_FF), F32), pltpu.VMEM((2, D_MODEL, D_FF), F32),
                            pltpu.VMEM((2, D_FF, D_MODEL), F32),
                            pltpu.VMEM((D_MODEL, D_FF), BF16), pltpu.VMEM((D_MODEL, D_FF), BF16),
                            pltpu.VMEM((D_FF, D_MODEL), BF16), pltpu.SemaphoreType.DMA((2,))]),
        out_shape=jax.ShapeDtypeStruct((n_tiles * rows, LANES), F32),
        compiler_params=_cparams(("arbitrary",)),
        name="moe_experts",
    )(*plan, x_sorted, w_gate, w_up, w_down)


def _combine_kernel(dr_ref, ys_hbm, x_ref, wt_ref, o_ref, g_s, sem, *, tm):
    i = pl.program_id(0)
    nt = pl.num_programs(0)
    slot = i % 2
    unroll = 8

    def issue(tile, dst_slot):
        def body(b, c):
            for u in range(unroll):
                r = b * unroll + u
                for k in range(TOP_K):
                    row = dr_ref[(tile * tm + r) * TOP_K + k]
                    pltpu.make_async_copy(ys_hbm.at[pl.ds(row * ROW_TILES, ROW_TILES), :],
                                          g_s.at[dst_slot, pl.ds((k * tm + r) * ROW_TILES, ROW_TILES), :],
                                          sem.at[dst_slot]).start(priority=k)
            return c
        lax.fori_loop(0, tm // unroll, body, 0)

    @pl.when(i == 0)
    def _():
        issue(0, 0)

    @pl.when(i + 1 < nt)
    def _():
        issue(i + 1, 1 - slot)

    pltpu.make_async_copy(ys_hbm.at[pl.ds(0, TOP_K * tm * ROW_TILES), :], g_s.at[slot], sem.at[slot]).wait()
    wt = wt_ref[...]
    y0 = _rows_to_tile(g_s.at[slot], 0, tm, ROW_TILES)
    y1 = _rows_to_tile(g_s.at[slot], tm * ROW_TILES, tm, ROW_TILES)
    o_ref[...] = x_ref[...] + (wt[:, 0:1] * y0 + wt[:, 1:2] * y1)


def _combine(dest_row, y_sorted, x1, wt, tm):
    t = x1.shape[0]
    tile = lambda i, dr: (i, 0)
    return pl.pallas_call(
        functools.partial(_combine_kernel, tm=tm),
        grid_spec=pltpu.PrefetchScalarGridSpec(
            num_scalar_prefetch=1,
            grid=(t // tm,),
            in_specs=[pl.BlockSpec(memory_space=pl.ANY), pl.BlockSpec((tm, D_MODEL), tile),
                      pl.BlockSpec((tm, LANES), tile)],
            out_specs=pl.BlockSpec((tm, D_MODEL), tile),
            scratch_shapes=[pltpu.VMEM((2, TOP_K * tm * ROW_TILES, LANES), F32), pltpu.SemaphoreType.DMA((2,))]),
        out_shape=jax.ShapeDtypeStruct((t, D_MODEL), F32),
        compiler_params=_cparams(("arbitrary",)),
        name="moe_combine",
    )(dest_row, y_sorted, x1, wt)


def _rotary_tables(pos):
    inv_freq = ROPE_THETA ** (-jnp.arange(0, ROT_DIM, 2, dtype=F32) / ROT_DIM)
    ang = pos[:, None] * inv_freq[None, :]
    cos, sin = jnp.cos(ang), jnp.sin(ang)
    rest = LANES - ROT_DIM
    ones = jnp.ones((pos.shape[0], rest), F32)
    zeros = jnp.zeros((pos.shape[0], rest), F32)
    return jnp.concatenate([cos, cos, ones], axis=1), jnp.concatenate([-sin, sin, zeros], axis=1)


def _projections(x, pos, norm1_g, w_in, q_norm_g, k_norm_g):
    t = x.shape[0]
    tm = min(TM_MAX, t)
    h = _rmsnorm(x, norm1_g, tm)
    tcos, tsin = _rotary_tables(pos)
    c_q = 2 * D_CONV
    c_k = c_q + QK_WIDTH
    c_v = c_k + QK_WIDTH
    c_g = c_v + V_WIDTH
    group = jnp.arange(TN) // HEAD_DIM
    group_ones = (group[:, None] == group[None, :]).astype(BF16)
    rot_in = lambda g: [g.reshape(1, LANES), tcos, tsin, group_ones]
    rot_specs = lambda order: [pl.BlockSpec((1, LANES), lambda a, b: (0, 0)),
                               pl.BlockSpec((tm, LANES), lambda a, b: ((a, b)[order], 0)),
                               pl.BlockSpec((tm, LANES), lambda a, b: ((a, b)[order], 0)),
                               pl.BlockSpec((TN, TN), lambda a, b: (0, 0))]
    u_lin = _glu_call(h, w_in, tm)
    q_b = _stationary_call(_q_kernel, h, w_in, c_q, QK_WIDTH, rot_in(q_norm_g), rot_specs(1), BF16, tm, TN, "proj_q")
    k_lin, k_b = _row_linear_call(_k_kernel, h, w_in, c_k, QK_WIDTH, rot_in(k_norm_g), rot_specs(0), tm, "proj_k")
    v_lin, v_b = _row_linear_call(_v_kernel, h, w_in, c_v, V_WIDTH, [], [], tm, "proj_v")
    gates = _stationary_call(_gate_kernel, h, w_in, c_g, 2 * D_MODEL, [], [], BF16, tm, TN_WIDE, "proj_gates")
    return u_lin, q_b, k_lin, k_b, v_lin, v_b, gates


def _layer(x_p, x_s, cache_k, cache_v, state_conv, page_table, lambda_init,
           norm1_g, w_in, conv_w, conv_b, conv_ln_g, conv_ln_b, w_conv_out, q_norm_g, k_norm_g,
           lambda_q1, lambda_k1, lambda_q2, lambda_k2, subln_g, w_attn_out, w_out, norm2_g,
           w_router_group, b_router_group, w_router_expert, b_router_expert, w_gate_e, w_up_e, w_down_e):
    batch, seq, _ = x_p.shape
    nb, n_new, _ = x_s.shape
    past = page_table.shape[1] * PAGE
    tp, ts = batch * seq, nb * n_new
    xp, xs = x_p.reshape(tp, D_MODEL), x_s.reshape(ts, D_MODEL)
    pos_p = jnp.tile(jnp.arange(seq, dtype=F32), batch)
    pos_s = jnp.tile(past + jnp.arange(n_new, dtype=F32), nb)
    u_p, q_p, kl_p, kb_p, vl_p, vb_p, gates_p = _projections(xp, pos_p, norm1_g, w_in, q_norm_g, k_norm_g)
    u_s, q_s, kl_s, kb_s, vl_s, vb_s, gates_s = _projections(xs, pos_s, norm1_g, w_in, q_norm_g, k_norm_g)

    u_p = u_p.reshape(batch, seq, CONV_TILES, LANES)
    u_s = u_s.reshape(nb, n_new, CONV_TILES, LANES)
    hist_p = jnp.concatenate([jnp.zeros((batch, HIST, CONV_TILES, LANES), F32), u_p], axis=1)
    hist_s = jnp.concatenate([state_conv.reshape(nb, HIST, CONV_TILES, LANES), u_s], axis=1)
    conv = lambda hist, name: _conv_branch(hist, conv_w, conv_b, conv_ln_g, conv_ln_b, name)
    yc_p = conv(hist_p, "conv_prompt").reshape(tp * CONV_TILES, LANES)
    yc_s = conv(hist_s, "conv_sample").reshape(ts * CONV_TILES, LANES)

    lam_rows = jnp.stack([lambda_q1, lambda_k1, lambda_q2, lambda_k2]).astype(F32)
    o_p = _prompt_attention(q_p, kb_p, vb_p, lam_rows, subln_g, batch, seq, lambda_init)
    pad_rows = lambda a: jnp.pad(a.reshape(nb, n_new, -1), ((0, 0), (0, NEW_ROWS - n_new), (0, 0)))
    o_s = _sample_attention(_query_rows(q_s.reshape(nb, n_new, QK_WIDTH)), pad_rows(kb_s), pad_rows(vb_s),
                            cache_k, cache_v, page_table, lam_rows, subln_g, lambda_init, n_new)
    o_s = o_s[:, :n_new].reshape(ts, V_WIDTH).astype(BF16)

    w_r = jnp.zeros((D_MODEL, LANES), F32).at[:, :N_GROUPS].set(w_router_group)
    w_r = w_r.at[:, N_GROUPS:N_GROUPS + N_EXPERTS].set(w_router_expert)
    w_hi = w_r.astype(BF16)
    w_lo = (w_r - w_hi.astype(F32)).astype(BF16)
    w_split = jnp.concatenate([w_hi, w_lo], axis=1)
    b_r = jnp.zeros((1, LANES), F32).at[0, :N_GROUPS].set(b_router_group)
    b_r = b_r.at[0, N_GROUPS:N_GROUPS + N_EXPERTS].set(b_router_expert)

    def tail(x, yc, o_n, gates):
        tm = min(TM_MAX, x.shape[0])
        m = _merge(yc, o_n, w_conv_out, w_attn_out, gates, min(512, tm))
        x1 = _out_projection(m, w_out, x, tm)
        return (x1,) + tuple(_router(x1, norm2_g, w_split, b_r, min(512, x.shape[0])))

    x1_p, hl_p, e_p, wt_p = tail(xp, yc_p, o_p, gates_p)
    x1_s, hl_s, e_s, wt_s = tail(xs, yc_s, o_s, gates_s)

    n_pairs = (tp + ts) * TOP_K
    n_tiles = -(-n_pairs // MOE_TM) + N_EXPERTS
    e_all = jnp.concatenate([e_p[:, :TOP_K], e_s[:, :TOP_K]], axis=0)
    dest_row, moe_plan = _routing_plan(e_all, n_tiles)
    x_sorted = _dispatch(dest_row, moe_plan[1], hl_p, hl_s, n_tiles)
    y_sorted = _moe(x_sorted, moe_plan, w_gate_e, w_up_e, w_down_e, n_tiles)
    y_p = _combine(dest_row[:tp * TOP_K], y_sorted, x1_p, wt_p, CMB_TM)
    y_s = _combine(dest_row[tp * TOP_K:], y_sorted, x1_s, wt_s, min(CMB_TM, ts))

    v_out = lambda a, lead: a.reshape(lead + (2, N_HEADS, LANES)).swapaxes(-3, -2).reshape(lead + (N_HEADS, V_DIM))
    return (y_p.reshape(batch, seq, D_MODEL), y_s.reshape(nb, n_new, D_MODEL),
            kl_p.reshape(batch, seq, N_HEADS, 2, HEAD_DIM), v_out(vl_p, (batch, seq)),
            hist_p[:, seq:].reshape(batch, HIST, D_CONV),
            kl_s.reshape(nb, n_new, N_HEADS, 2, HEAD_DIM), v_out(vl_s, (nb, n_new)),
            hist_s[:, n_new:].reshape(nb, HIST, D_CONV))


def kernel(x_prompt, x_sample, cache_k, cache_v, state_conv, page_table, norm1_g, w_in, conv_w, conv_b, conv_ln_g, conv_ln_b, w_conv_out, q_norm_g, k_norm_g, lambda_q1, lambda_k1, lambda_q2, lambda_k2, subln_g, w_attn_out, w_out, norm2_g, w_router_group, b_router_group, w_router_expert, b_router_expert, w_gate_e, w_up_e, w_down_e):
    depth = norm1_g.shape[0]
    x_p, x_s = x_prompt, x_sample
    outs = [[] for _ in range(6)]
    for l in range(depth):
        lambda_init = 0.8 - 0.6 * math.exp(-0.3 * l)
        x_p, x_s, *rest = _layer(
            x_p, x_s, cache_k[l], cache_v[l], state_conv[l], page_table, lambda_init,
            norm1_g[l], w_in[l], conv_w[l], conv_b[l], conv_ln_g[l], conv_ln_b[l], w_conv_out[l],
            q_norm_g[l], k_norm_g[l], lambda_q1[l], lambda_k1[l], lambda_q2[l], lambda_k2[l], subln_g[l],
            w_attn_out[l], w_out[l], norm2_g[l], w_router_group[l], b_router_group[l],
            w_router_expert[l], b_router_expert[l], w_gate_e[l], w_up_e[l], w_down_e[l])
        for acc, val in zip(outs, rest):
            acc.append(val)
    return (x_p, x_s) + tuple(jnp.stack(o) for o in outs)
```

```python
import functools
import math

import jax
import jax.numpy as jnp
from jax import lax
from jax.experimental import pallas as pl
from jax.experimental.pallas import tpu as pltpu

F32 = jnp.float32
BF16 = jnp.bfloat16

D_MODEL = 2048
N_HEADS = 8
HEAD_DIM = 128
V_DIM = 2 * HEAD_DIM
QK_WIDTH = N_HEADS * 2 * HEAD_DIM
V_WIDTH = N_HEADS * V_DIM
ROT_DIM = HEAD_DIM // 4
ROPE_THETA = 500000.0
ATTN_SCALE = HEAD_DIM ** -0.5
LOG2E = math.log2(math.e)
NEG_INF = -1e30
D_CONV = D_MODEL // 2
CONV_WIDTH = 31
HIST = CONV_WIDTH - 1
N_GROUPS = 4
EXPERTS_PER_GROUP = 8
N_EXPERTS = N_GROUPS * EXPERTS_PER_GROUP
TOP_K = 2
D_FF = D_MODEL // 4
EPS = 1e-6
PAGE = 128

LANES = 128
SUBLANES = 8
ROW_TILES = D_MODEL // LANES
CONV_TILES = D_CONV // LANES
VMEM_LIMIT = 56 * 1024 * 1024

TM_MAX = 1024
TN = 512
TN_WIDE = 1024
ATTN_ROWS = 256
PAGES_PER_STEP = 8
QROWS = 2 * 4 * SUBLANES
NEW_ROWS = 16
MOE_TM = 256
CMB_TM = 256
CONV_ROWS = 32


def _cparams(sem):
    return pltpu.CompilerParams(dimension_semantics=sem, vmem_limit_bytes=VMEM_LIMIT)


def _rows_to_tile(buf, base, rows, tiles):
    return jnp.concatenate([buf[pl.ds(base + c, rows, stride=tiles), :] for c in range(tiles)], axis=1)


def _rmsnorm_kernel(x_ref, g_ref, o_ref):
    x = x_ref[...]
    ms = jnp.mean(x * x, axis=-1, keepdims=True)
    o_ref[...] = ((x * lax.rsqrt(ms + EPS)) * g_ref[...]).astype(o_ref.dtype)


def _rmsnorm(x, g, tm):
    t, d = x.shape
    return pl.pallas_call(
        _rmsnorm_kernel,
        grid=(t // tm,),
        in_specs=[pl.BlockSpec((tm, d), lambda i: (i, 0)), pl.BlockSpec((1, d), lambda i: (0, 0))],
        out_specs=pl.BlockSpec((tm, d), lambda i: (i, 0)),
        out_shape=jax.ShapeDtypeStruct((t, d), BF16),
        compiler_params=_cparams(("arbitrary",)),
        name="rmsnorm1",
    )(x, g.reshape(1, d))


def _cache_weight(w_ref, w_s):
    @pl.when(pl.program_id(1) == 0)
    def _():
        w_s[...] = w_ref[...].astype(BF16)


def _store_row_linear(o_ref, val, first_tile, tiles, row_of_tile=None):
    rows = val.shape[0]
    for c in range(val.shape[1] // LANES):
        tile = first_tile + c
        dst = tile if row_of_tile is None else row_of_tile(tile)
        o_ref[pl.ds(dst, rows, stride=tiles), :] = val[:, c * LANES:(c + 1) * LANES]


def _glu_kernel(h_ref, wa_ref, wb_ref, u_ref):
    h = h_ref[...]
    a = jnp.dot(h, wa_ref[...], preferred_element_type=F32)
    b = jnp.dot(h, wb_ref[...], preferred_element_type=F32)
    _store_row_linear(u_ref, a * jax.nn.sigmoid(b), pl.program_id(1) * (TN // LANES), CONV_TILES)


def _glu_call(h, w, tm):
    t, k = h.shape
    nj = D_CONV // TN
    return pl.pallas_call(
        _glu_kernel,
        grid=(t // tm, nj),
        in_specs=[pl.BlockSpec((tm, k), lambda i, j: (i, 0)), pl.BlockSpec((k, TN), lambda i, j: (0, j)),
                  pl.BlockSpec((k, TN), lambda i, j: (0, nj + j))],
        out_specs=pl.BlockSpec((tm * CONV_TILES, LANES), lambda i, j: (i, 0)),
        out_shape=jax.ShapeDtypeStruct((t * CONV_TILES, LANES), F32),
        compiler_params=_cparams(("arbitrary", "arbitrary")),
        name="proj_glu",
    )(h, w, w)


def _head_norm_rotary(acc, ones_ref, gain, tcos, tsin):
    ssq = jnp.dot((acc * acc).astype(BF16), ones_ref[...], preferred_element_type=F32)
    tn = acc * lax.rsqrt(ssq * (1.0 / HEAD_DIM) + EPS)
    lane = lax.broadcasted_iota(jnp.int32, (acc.shape[0], LANES), 1)
    half = ROT_DIM // 2
    outs = []
    for g in range(acc.shape[1] // LANES):
        t = tn[:, g * LANES:(g + 1) * LANES] * gain
        partner = jnp.where(lane < half, pltpu.roll(t, LANES - half, 1), pltpu.roll(t, half, 1))
        outs.append(t * tcos + partner * tsin)
    return jnp.concatenate(outs, axis=1)


def _q_kernel(h_ref, w_ref, g_ref, tc_ref, ts_ref, ones_ref, ob_ref):
    acc = jnp.dot(h_ref[...], w_ref[...], preferred_element_type=F32)
    y = _head_norm_rotary(acc, ones_ref, g_ref[...], tc_ref[...], ts_ref[...])
    ob_ref[...] = (y * (ATTN_SCALE * LOG2E)).astype(BF16)


def _gate_kernel(h_ref, w_ref, o_ref):
    o_ref[...] = jax.nn.sigmoid(jnp.dot(h_ref[...], w_ref[...], preferred_element_type=F32)).astype(o_ref.dtype)


def _stationary_call(kernel, h, w, col0, width, extra_in, extra_specs, out_dtype, tm, tn, name):
    t, k = h.shape
    return pl.pallas_call(
        kernel,
        grid=(width // tn, t // tm),
        in_specs=[pl.BlockSpec((tm, k), lambda j, i: (i, 0)),
                  pl.BlockSpec((k, tn), lambda j, i: (0, col0 // tn + j))] + extra_specs,
        out_specs=pl.BlockSpec((tm, tn), lambda j, i: (i, j)),
        out_shape=jax.ShapeDtypeStruct((t, width), out_dtype),
        compiler_params=_cparams(("arbitrary", "arbitrary")),
        name=name,
    )(h, w, *extra_in)


def _k_kernel(h_ref, w_ref, g_ref, tc_ref, ts_ref, ones_ref, of_ref, ob_ref):
    acc = jnp.dot(h_ref[...], w_ref[...], preferred_element_type=F32)
    y = _head_norm_rotary(acc, ones_ref, g_ref[...], tc_ref[...], ts_ref[...])
    _store_row_linear(of_ref, y, pl.program_id(1) * (TN // LANES), ROW_TILES)
    ob_ref[...] = y.astype(BF16)


def _v_kernel(h_ref, w_ref, of_ref, ob_ref):
    acc = jnp.dot(h_ref[...], w_ref[...], preferred_element_type=F32)
    _store_row_linear(of_ref, acc, pl.program_id(1) * (TN // LANES), ROW_TILES,
                      row_of_tile=lambda tile: (tile % 2) * N_HEADS + tile // 2)
    ob_ref[...] = acc.astype(BF16)


def _row_linear_call(kernel, h, w, col0, width, extra_in, extra_specs, tm, name):
    t, k = h.shape
    tiles = width // LANES
    return pl.pallas_call(
        kernel,
        grid=(t // tm, width // TN),
        in_specs=[pl.BlockSpec((tm, k), lambda i, j: (i, 0)),
                  pl.BlockSpec((k, TN), lambda i, j: (0, col0 // TN + j))] + extra_specs,
        out_specs=[pl.BlockSpec((tm * tiles, LANES), lambda i, j: (i, 0)), pl.BlockSpec((tm, TN), lambda i, j: (i, j))],
        out_shape=[jax.ShapeDtypeStruct((t * tiles, LANES), F32), jax.ShapeDtypeStruct((t, width), BF16)],
        compiler_params=_cparams(("arbitrary", "arbitrary")),
        name=name,
    )(h, w, *extra_in)


def _conv_kernel(u_ref, hist_ref, w_ref, b_ref, g_ref, beta_ref, o_ref, win_s, *, length, seqs):
    rows = min(CONV_ROWS, length)
    trips = length // rows

    def finish(acc, s, t0):
        y = acc + b_ref[...]
        mu = jnp.mean(y, axis=(1, 2), keepdims=True)
        yc = y - mu
        var = jnp.mean(yc * yc, axis=(1, 2), keepdims=True)
        z = yc * lax.rsqrt(var + EPS) * g_ref[...] + beta_ref[...]
        o_ref[s, pl.ds(t0, rows)] = z * jax.nn.sigmoid(z)

    def per_sequence(s, carry):
        win_s[0:HIST] = hist_ref[s]
        win_s[HIST:HIST + rows] = u_ref[s, 0:rows]
        acc = jnp.zeros((rows, CONV_TILES, LANES), F32)
        for j in range(CONV_WIDTH):
            acc = acc + win_s[j:j + rows] * w_ref[j]
        finish(acc, s, 0)

        def step(i, c):
            t0 = i * rows
            acc = jnp.zeros((rows, CONV_TILES, LANES), F32)
            for j in range(CONV_WIDTH):
                acc = acc + u_ref[s, pl.ds(t0 - HIST + j, rows)] * w_ref[j]
            finish(acc, s, t0)
            return c

        if trips > 1:
            rest = trips - 1
            lax.fori_loop(1, trips, step, 0, unroll=next(u for u in (2, 3, 1) if rest % u == 0))
        return carry

    if seqs == 1:
        per_sequence(0, 0)
    else:
        lax.fori_loop(0, seqs, per_sequence, 0)


def _conv_branch(u, hist, conv_w, conv_b, ln_g, ln_b, seqs, name):
    b, length = u.shape[:2]
    assert length <= CONV_ROWS or CONV_ROWS >= HIST, "blocks after the first must start past the carried history"
    tile = lambda a: a.reshape(-1, CONV_TILES, LANES)
    whole = lambda n: pl.BlockSpec((n, CONV_TILES, LANES), lambda i: (0, 0, 0))
    per_step = lambda n: pl.BlockSpec((seqs, n, CONV_TILES, LANES), lambda i: (i, 0, 0, 0))
    return pl.pallas_call(
        functools.partial(_conv_kernel, length=length, seqs=seqs),
        grid=(b // seqs,),
        in_specs=[per_step(length), per_step(HIST), whole(CONV_WIDTH), whole(1), whole(1), whole(1)],
        out_specs=per_step(length),
        out_shape=jax.ShapeDtypeStruct((b, length, CONV_TILES, LANES), F32),
        scratch_shapes=[pltpu.VMEM((HIST + min(CONV_ROWS, length), CONV_TILES, LANES), F32)],
        compiler_params=_cparams(("arbitrary",)),
        name=name,
    )(u, hist, tile(conv_w), tile(conv_b), tile(ln_g), tile(ln_b))


def _lambda_value(lam_ref, lambda_init):
    v = lam_ref[...]
    s1 = jnp.sum(v[0:1] * v[1:2], axis=-1, keepdims=True)
    s2 = jnp.sum(v[2:3] * v[3:4], axis=-1, keepdims=True)
    return jnp.exp(s1) - jnp.exp(s2) + lambda_init


def _prompt_attn_kernel(q_ref, k_ref, v_ref, lam_ref, sg_ref, o_ref, *, lambda_init, seq):
    rb = ATTN_ROWS
    lam = _lambda_value(lam_ref, lambda_init)
    row = lax.broadcasted_iota(jnp.int32, (rb, rb), 0)
    col = lax.broadcasted_iota(jnp.int32, (rb, rb), 1)
    for r in range(seq // rb):
        rows = pl.ds(r * rb, rb)
        kc = (r + 1) * rb
        v = v_ref[0:kc, :]
        outs = []
        for c in range(2):
            q = q_ref[rows, c * HEAD_DIM:(c + 1) * HEAD_DIM]
            k = k_ref[0:kc, c * HEAD_DIM:(c + 1) * HEAD_DIM]
            s = lax.dot_general(q, k, (((1,), (1,)), ((), ())), preferred_element_type=F32)
            diag = jnp.where(col <= row, s[:, kc - rb:], NEG_INF)
            s = diag if r == 0 else jnp.concatenate([s[:, :kc - rb], diag], axis=1)
            p = jnp.exp2(s - jnp.max(s, axis=-1, keepdims=True))
            l = jnp.sum(p, axis=-1, keepdims=True)
            outs.append(jnp.dot(p.astype(BF16), v, preferred_element_type=F32) / l)
        o = outs[0] - lam * outs[1]
        ms = jnp.mean(o * o, axis=-1, keepdims=True)
        o_ref[rows, :] = (((o * lax.rsqrt(ms + EPS)) * sg_ref[...]) * (1.0 - lambda_init)).astype(o_ref.dtype)


def _prompt_attention(q, k, v, lam_rows, subln_g, batch, seq, lambda_init):
    blk = lambda b, h: (b, h)
    return pl.pallas_call(
        functools.partial(_prompt_attn_kernel, lambda_init=lambda_init, seq=seq),
        grid=(batch, N_HEADS),
        in_specs=[pl.BlockSpec((seq, V_DIM), blk), pl.BlockSpec((seq, V_DIM), blk), pl.BlockSpec((seq, V_DIM), blk),
                  pl.BlockSpec((4, HEAD_DIM), lambda b, h: (0, 0)), pl.BlockSpec((1, V_DIM), lambda b, h: (0, 0))],
        out_specs=pl.BlockSpec((seq, V_DIM), blk),
        out_shape=jax.ShapeDtypeStruct((batch * seq, V_WIDTH), BF16),
        compiler_params=_cparams(("arbitrary", "arbitrary")),
        name="prompt_attention",
    )(q, k, v, lam_rows, subln_g.reshape(1, V_DIM))


def _page_tiles(x):
    tiles = jnp.swapaxes(x, 0, 1)
    return [tiles[i].astype(BF16) for i in range(SUBLANES)]


def _sample_attn_kernel(pt_ref, *refs, n_new, lambda_init):
    g = PAGES_PER_STEP
    k_refs, v_refs = refs[:g], refs[g:2 * g]
    q_ref, kn_ref, vn_ref, lam_ref, sg_ref, o_ref, m_s, l_s, acc_s = refs[2 * g:]
    j = pl.program_id(1)
    q2 = q_ref[0]
    nt_dims = (((1,), (1,)), ((), ()))

    @pl.when(j == 0)
    def _():
        m_s[...] = jnp.full(m_s.shape, NEG_INF, F32)
        l_s[...] = jnp.zeros(l_s.shape, F32)
        acc_s[...] = jnp.zeros(acc_s.shape, F32)

    def online_update(s, values):
        m_old = m_s[...]
        m_new = jnp.maximum(m_old, jnp.max(s, axis=-1, keepdims=True))
        alpha = jnp.exp2(m_old - m_new)
        p = jnp.exp2(s - m_new)
        l_s[...] = alpha * l_s[...] + jnp.sum(p, axis=-1, keepdims=True)
        pv = None
        c0 = 0
        for n, v in values:
            part = jnp.dot(p[:, c0:c0 + n].astype(BF16), v, preferred_element_type=F32)
            pv = part if pv is None else pv + part
            c0 += n
        acc_s[...] = alpha * acc_s[...] + pv
        m_s[...] = m_new

    s_parts, values = [], []
    for p in range(g):
        kfull = jnp.concatenate(_page_tiles(k_refs[p][:, 0]) + _page_tiles(k_refs[p][:, 1]), axis=1)
        s_parts.append(lax.dot_general(q2, kfull, nt_dims, preferred_element_type=F32))
        t0, t1 = _page_tiles(v_refs[p][:, :, :LANES]), _page_tiles(v_refs[p][:, :, LANES:])
        values.append((PAGE, jnp.concatenate([t for pair in zip(t0, t1) for t in pair], axis=1)))
    online_update(jnp.concatenate(s_parts, axis=1), values)

    @pl.when(j == pl.num_programs(1) - 1)
    def _():
        kn = kn_ref[0]
        s = lax.dot_general(q2, kn, nt_dims, preferred_element_type=F32)
        key = lax.broadcasted_iota(jnp.int32, s.shape, 1)
        qry = (lax.broadcasted_iota(jnp.int32, s.shape, 0) % (4 * SUBLANES)) // SUBLANES
        s = jnp.where((key <= qry) & (key < n_new), s, NEG_INF)
        online_update(s, [(kn.shape[0], vn_ref[0])])
        lam = _lambda_value(lam_ref, lambda_init)
        half = QROWS // 2
        acc, l = acc_s[...], l_s[...]
        o_all = acc[:half] / l[:half] - lam * (acc[half:] / l[half:])
        head = lax.broadcasted_iota(jnp.int32, (SUBLANES, V_WIDTH), 0)
        own = head == lax.broadcasted_iota(jnp.int32, (SUBLANES, V_WIDTH), 1) // V_DIM
        rows = []
        for q in range(n_new):
            o = jnp.where(own, o_all[q * SUBLANES:(q + 1) * SUBLANES, :], 0.0)
            ms = jnp.sum(o * o, axis=-1, keepdims=True) / V_DIM
            o = jnp.sum(o * lax.rsqrt(ms + EPS), axis=0, keepdims=True)
            rows.append((o * sg_ref[...]) * (1.0 - lambda_init))
        rows.append(jnp.zeros((SUBLANES - n_new, V_WIDTH), F32))
        o_ref[0] = jnp.concatenate(rows, axis=0)


def _sample_attention(q_rows, k_new, v_new, cache_k, cache_v, page_table, lam_rows, subln_g, lambda_init, n_new):
    nb, n_pages = page_table.shape
    g = PAGES_PER_STEP
    n_pool = cache_k.shape[0]
    ck = cache_k.reshape(n_pool * PAGE, 2, SUBLANES, HEAD_DIM)
    cv = cache_v.reshape(n_pool * PAGE, N_HEADS, V_DIM)
    pt = page_table.reshape(-1)
    kmap = lambda p: (lambda b, j, pt: (pt[b * n_pages + j * g + p], 0, 0, 0))
    vmap = lambda p: (lambda b, j, pt: (pt[b * n_pages + j * g + p], 0, 0))
    k_specs = [pl.BlockSpec((PAGE, 2, SUBLANES, HEAD_DIM), kmap(p)) for p in range(g)]
    v_specs = [pl.BlockSpec((PAGE, N_HEADS, V_DIM), vmap(p)) for p in range(g)]
    new_rows = k_new.shape[1]
    per_b = lambda shape: pl.BlockSpec((1,) + shape, lambda b, j, pt: (b, 0, 0))
    const = lambda shape: pl.BlockSpec(shape, lambda b, j, pt: (0, 0))
    tiled_g = jnp.tile(subln_g.reshape(1, V_DIM), (1, N_HEADS))
    return pl.pallas_call(
        functools.partial(_sample_attn_kernel, n_new=n_new, lambda_init=lambda_init),
        grid_spec=pltpu.PrefetchScalarGridSpec(
            num_scalar_prefetch=1,
            grid=(nb, n_pages // g),
            in_specs=k_specs + v_specs + [per_b((QROWS, QK_WIDTH)), per_b((new_rows, QK_WIDTH)),
                                          per_b((new_rows, V_WIDTH)), const((4, HEAD_DIM)), const((1, V_WIDTH))],
            out_specs=per_b((SUBLANES, V_WIDTH)),
            scratch_shapes=[pltpu.VMEM((QROWS, 1), F32), pltpu.VMEM((QROWS, 1), F32), pltpu.VMEM((QROWS, V_WIDTH), F32)]),
        out_shape=jax.ShapeDtypeStruct((nb, SUBLANES, V_WIDTH), F32),
        compiler_params=_cparams(("arbitrary", "arbitrary")),
        name="sample_attention",
    )(pt, *([ck] * g), *([cv] * g), q_rows, k_new, v_new, lam_rows, tiled_g)


def _query_rows(q_s):
    n_new = q_s.shape[1]
    r = jnp.arange(QROWS)
    r_c, r_q, r_h = r // (4 * SUBLANES), (r % (4 * SUBLANES)) // SUBLANES, r % SUBLANES
    col_hc = jnp.arange(QK_WIDTH) // HEAD_DIM
    keep = (col_hc[None, :] == (r_h * 2 + r_c)[:, None]) & (r_q < n_new)[:, None]
    picked = q_s[:, jnp.minimum(r_q, n_new - 1), :]
    return jnp.where(keep[None], picked, jnp.zeros((), q_s.dtype))


def _merge_kernel(yc_ref, ya_ref, wc_ref, wa_ref, gc_ref, ga_ref, m_ref, wc_s, wa_s):
    _cache_weight(wc_ref, wc_s)
    _cache_weight(wa_ref, wa_s)
    conv = _rows_to_tile(yc_ref, 0, ya_ref.shape[0], CONV_TILES).astype(BF16)
    yc = jnp.dot(conv, wc_s[...], preferred_element_type=F32)
    ya = jnp.dot(ya_ref[...], wa_s[...], preferred_element_type=F32)
    m_ref[...] = (gc_ref[...] * yc + ga_ref[...] * ya).astype(m_ref.dtype)


def _merge(yconv_lin, o_n, w_conv_out, w_attn_out, gates, tm):
    t = o_n.shape[0]
    tn = TN_WIDE
    nj, ni = D_MODEL // tn, t // tm
    return pl.pallas_call(
        _merge_kernel,
        grid=(nj, ni),
        in_specs=[pl.BlockSpec((tm * CONV_TILES, LANES), lambda j, i: (i, 0)),
                  pl.BlockSpec((tm, V_WIDTH), lambda j, i: (i, 0)),
                  pl.BlockSpec((D_CONV, tn), lambda j, i: (0, j)), pl.BlockSpec((V_WIDTH, tn), lambda j, i: (0, j)),
                  pl.BlockSpec((tm, tn), lambda j, i: (i, j)), pl.BlockSpec((tm, tn), lambda j, i: (i, nj + j))],
        out_specs=pl.BlockSpec((tm, tn), lambda j, i: (i, j)),
        out_shape=jax.ShapeDtypeStruct((t, D_MODEL), BF16),
        scratch_shapes=[pltpu.VMEM((D_CONV, tn), BF16), pltpu.VMEM((V_WIDTH, tn), BF16)],
        compiler_params=_cparams(("arbitrary", "arbitrary")),
        name="merge",
    )(yconv_lin, o_n, w_conv_out, w_attn_out, gates, gates)


def _residual_kernel(m_ref, w_ref, x_ref, o_ref, w_s):
    _cache_weight(w_ref, w_s)
    o_ref[...] = x_ref[...] + jnp.dot(m_ref[...], w_s[...], preferred_element_type=F32)


def _out_projection(m, w_out, x, tm):
    t = m.shape[0]
    tn = TN_WIDE
    return pl.pallas_call(
        _residual_kernel,
        grid=(D_MODEL // tn, t // tm),
        in_specs=[pl.BlockSpec((tm, D_MODEL), lambda j, i: (i, 0)), pl.BlockSpec((D_MODEL, tn), lambda j, i: (0, j)),
                  pl.BlockSpec((tm, tn), lambda j, i: (i, j))],
        out_specs=pl.BlockSpec((tm, tn), lambda j, i: (i, j)),
        out_shape=jax.ShapeDtypeStruct((t, D_MODEL), F32),
        scratch_shapes=[pltpu.VMEM((D_MODEL, tn), BF16)],
        compiler_params=_cparams(("arbitrary", "arbitrary")),
        name="out_projection",
    )(m, w_out, x)


def _router_kernel(x_ref, g_ref, w_ref, b_ref, hl_ref, e_ref, wt_ref):
    x = x_ref[...]
    ms = jnp.mean(x * x, axis=-1, keepdims=True)
    h = (x * lax.rsqrt(ms + EPS)) * g_ref[...]
    _store_row_linear(hl_ref, h, 0, ROW_TILES)
    h_hi = h.astype(BF16)
    h_lo = (h - h_hi.astype(F32)).astype(BF16)
    both = jnp.dot(h_hi, w_ref[...], preferred_element_type=F32)
    logits = both[:, :LANES] + both[:, LANES:] + jnp.dot(h_lo, w_ref[:, :LANES], preferred_element_type=F32)
    logits = logits + b_ref[...]
    lane = lax.broadcasted_iota(jnp.int32, logits.shape, 1)
    neg = -jnp.inf
    big = jnp.int32(LANES)
    lg = jnp.where(lane < N_GROUPS, logits, neg)
    eg = jnp.exp(lg - jnp.max(lg, axis=-1, keepdims=True))
    pg = eg / jnp.sum(eg, axis=-1, keepdims=True)
    p_top = jnp.max(pg, axis=-1, keepdims=True)
    g_idx = jnp.min(jnp.where(pg == p_top, lane, big), axis=-1, keepdims=True)
    in_group = (lane >= N_GROUPS) & (lane < N_GROUPS + N_EXPERTS) & (((lane - N_GROUPS) // EXPERTS_PER_GROUP) == g_idx)
    le = jnp.where(in_group, logits, neg)
    ee = jnp.exp(le - jnp.max(le, axis=-1, keepdims=True))
    pe = jnp.where(in_group, ee / jnp.sum(ee, axis=-1, keepdims=True), -1.0)
    w1 = jnp.max(pe, axis=-1, keepdims=True)
    i1 = jnp.min(jnp.where(pe == w1, lane, big), axis=-1, keepdims=True)
    pe2 = jnp.where(lane == i1, -1.0, pe)
    w2 = jnp.max(pe2, axis=-1, keepdims=True)
    i2 = jnp.min(jnp.where(pe2 == w2, lane, big), axis=-1, keepdims=True)
    denom = w1 + w2
    e_ref[...] = jnp.where(lane == 0, i1 - N_GROUPS, jnp.where(lane == 1, i2 - N_GROUPS, 0))
    wt_ref[...] = jnp.where(lane == 0, w1 / denom * p_top, jnp.where(lane == 1, w2 / denom * p_top, 0.0))


def _router(x1, norm2_g, w_router, b_router, tm):
    t = x1.shape[0]
    return pl.pallas_call(
        _router_kernel,
        grid=(t // tm,),
        in_specs=[pl.BlockSpec((tm, D_MODEL), lambda i: (i, 0)), pl.BlockSpec((1, D_MODEL), lambda i: (0, 0)),
                  pl.BlockSpec((D_MODEL, 2 * LANES), lambda i: (0, 0)), pl.BlockSpec((1, LANES), lambda i: (0, 0))],
        out_specs=[pl.BlockSpec((tm * ROW_TILES, LANES), lambda i: (i, 0)),
                   pl.BlockSpec((tm, LANES), lambda i: (i, 0)), pl.BlockSpec((tm, LANES), lambda i: (i, 0))],
        out_shape=[jax.ShapeDtypeStruct((t * ROW_TILES, LANES), F32),
                   jax.ShapeDtypeStruct((t, LANES), jnp.int32), jax.ShapeDtypeStruct((t, LANES), F32)],
        compiler_params=_cparams(("arbitrary",)),
        name="router",
    )(x1, norm2_g.reshape(1, D_MODEL), w_router, b_router)


def _routing_plan(expert_idx, n_tiles):
    e_flat = expert_idx.reshape(-1)
    onehot = (e_flat[:, None] == jnp.arange(N_EXPERTS, dtype=jnp.int32)[None, :]).astype(jnp.int32)
    csum = jnp.cumsum(onehot, axis=0)
    counts = csum[-1]
    rank = jnp.sum(onehot * csum, axis=1) - 1
    tiles_e = (counts + MOE_TM - 1) // MOE_TM
    tile_end = jnp.cumsum(tiles_e)
    tile_start = tile_end - tiles_e
    dest_row = jnp.sum(onehot * tile_start[None, :], axis=1) * MOE_TM + rank
    n_used = tile_end[-1]
    tile_id = jnp.arange(n_tiles, dtype=jnp.int32)
    te = jnp.sum((tile_end[None, :] <= jnp.minimum(tile_id, n_used - 1)[:, None]).astype(jnp.int32), axis=1)
    te = jnp.minimum(te, N_EXPERTS - 1)
    left = counts[te] - (tile_id - tile_start[te]) * MOE_TM
    tile_rows = jnp.where(tile_id < n_used, jnp.clip(left, 0, MOE_TM), 0)
    first = (tile_id < n_used) & (tile_id == tile_start[te])
    slot = (jnp.cumsum(first.astype(jnp.int32)) - 1) % 2
    ids = jnp.arange(N_EXPERTS, dtype=jnp.int32)
    later_used = (ids[None, :] > ids[:, None]) & (counts > 0)[None, :]
    next_used = jnp.min(jnp.where(later_used, ids[None, :], N_EXPERTS), axis=1)
    next_expert = jnp.where(next_used[te] < N_EXPERTS, next_used[te], -1)
    i32 = lambda a: a.astype(jnp.int32)
    moe_plan = (i32(te), i32(tile_rows), i32(n_used).reshape(1), i32(first), i32(slot), i32(next_expert))
    return i32(dest_row), moe_plan


def _dispatch_kernel(dr_ref, tr_ref, hp_ref, hs_ref, xs_hbm, zero_s, sem, zsem, *, tile_p, p_steps, n_sample, n_tiles):
    i = pl.program_id(0)
    unroll = 8

    def scatter(src_ref, n_tok, pair0):
        def body(b, c):
            for u in range(unroll):
                r = b * unroll + u
                for k in range(TOP_K):
                    dst = dr_ref[pair0 + r * TOP_K + k]
                    pltpu.make_async_copy(src_ref.at[pl.ds(r * ROW_TILES, ROW_TILES), :],
                                          xs_hbm.at[pl.ds(dst * ROW_TILES, ROW_TILES), :], sem).start(priority=k)
            return c

        lax.fori_loop(0, n_tok // unroll, body, 0)
        for _ in range(TOP_K):
            pltpu.make_async_copy(src_ref, xs_hbm.at[pl.ds(0, n_tok * ROW_TILES), :], sem).wait()

    @pl.when(i < p_steps)
    def _():
        scatter(hp_ref, tile_p, i * (tile_p * TOP_K))

    @pl.when(i == p_steps)
    def _():
        scatter(hs_ref, n_sample, p_steps * tile_p * TOP_K)
        zero_s[...] = jnp.zeros(zero_s.shape, F32)

        def pad_body(t, carry):
            n = tr_ref[t]

            @pl.when(n < MOE_TM)
            def _():
                rows = (MOE_TM - n) * ROW_TILES
                cp = pltpu.make_async_copy(zero_s.at[pl.ds(0, rows), :],
                                           xs_hbm.at[pl.ds((t * MOE_TM + n) * ROW_TILES, rows), :], zsem)
                cp.start()
                cp.wait()
            return carry

        lax.fori_loop(0, n_tiles, pad_body, 0)


def _dispatch(dest_row, tile_rows, h_lin_p, h_lin_s, n_tiles):
    n_prompt, n_sample = h_lin_p.shape[0] // ROW_TILES, h_lin_s.shape[0] // ROW_TILES
    tile_p = min(512, n_prompt)
    p_steps = n_prompt // tile_p
    return pl.pallas_call(
        functools.partial(_dispatch_kernel, tile_p=tile_p, p_steps=p_steps, n_sample=n_sample, n_tiles=n_tiles),
        grid_spec=pltpu.PrefetchScalarGridSpec(
            num_scalar_prefetch=2,
            grid=(p_steps + 1,),
            in_specs=[pl.BlockSpec((tile_p * ROW_TILES, LANES), lambda i, dr, tr: (jnp.minimum(i, p_steps - 1), 0)),
                      pl.BlockSpec((n_sample * ROW_TILES, LANES), lambda i, dr, tr: (0, 0))],
            out_specs=pl.BlockSpec(memory_space=pl.ANY),
            scratch_shapes=[pltpu.VMEM((MOE_TM * ROW_TILES, LANES), F32), pltpu.SemaphoreType.DMA(()),
                            pltpu.SemaphoreType.DMA(())]),
        out_shape=jax.ShapeDtypeStruct((n_tiles * MOE_TM * ROW_TILES, LANES), F32),
        compiler_params=_cparams(("arbitrary",)),
        name="moe_dispatch",
    )(dest_row, tile_rows, h_lin_p, h_lin_s)


def _moe_kernel(te_ref, tr_ref, nu_ref, first_ref, slot_ref, next_ref, x_ref, wg_hbm, wu_hbm, wd_hbm, y_ref,
                wg_f, wu_f, wd_f, wg_s, wu_s, wd_s, sem):
    i = pl.program_id(0)
    n = tr_ref[i]

    def weight_copies(e, slot):
        return [pltpu.make_async_copy(src.at[e], dst.at[slot], sem.at[slot])
                for src, dst in ((wg_hbm, wg_f), (wu_hbm, wu_f), (wd_hbm, wd_f))]

    @pl.when(i == 0)
    def _():
        for cp in weight_copies(te_ref[0], 0):
            cp.start()

    @pl.when(n == 0)
    def _():
        y_ref[...] = jnp.zeros(y_ref.shape, F32)

    @pl.when(n > 0)
    def _():
        @pl.when(first_ref[i] == 1)
        def _():
            slot = slot_ref[i]
            for cp in weight_copies(te_ref[i], slot):
                cp.wait()
            wg_s[...] = wg_f[slot].astype(BF16)
            wu_s[...] = wu_f[slot].astype(BF16)
            wd_s[...] = wd_f[slot].astype(BF16)

            @pl.when(next_ref[i] >= 0)
            def _():
                for cp in weight_copies(next_ref[i], 1 - slot):
                    cp.start()

        x = _rows_to_tile(x_ref, 0, MOE_TM, ROW_TILES).astype(BF16)
        hg = jnp.dot(x, wg_s[...], preferred_element_type=F32)
        hu = jnp.dot(x, wu_s[...], preferred_element_type=F32)
        act = ((hg * jax.nn.sigmoid(hg)) * hu).astype(BF16)
        y = jnp.dot(act, wd_s[...], preferred_element_type=F32)
        _store_row_linear(y_ref, y, 0, ROW_TILES)


def _moe(x_sorted, plan, w_gate, w_up, w_down, n_tiles):
    rows = MOE_TM * ROW_TILES
    any_spec = pl.BlockSpec(memory_space=pl.ANY)
    return pl.pallas_call(
        _moe_kernel,
        grid_spec=pltpu.PrefetchScalarGridSpec(
            num_scalar_prefetch=len(plan),
            grid=(n_tiles,),
            in_specs=[pl.BlockSpec((rows, LANES), lambda i, te, tr, nu, *_: (jnp.minimum(i, nu[0] - 1), 0)),
                      any_spec, any_spec, any_spec],
            out_specs=pl.BlockSpec((rows, LANES), lambda i, *_: (i, 0)),
            scratch_shapes=[pltpu.VMEM((2, D_MODEL, D_FF), F32), pltpu.VMEM((2, D_MODEL, D_FF), F32),
                            pltpu.VMEM((2, D_FF, D_MODEL), F32),
                            pltpu.VMEM((D_MODEL, D_FF), BF16), pltpu.VMEM((D_MODEL, D_FF), BF16),
                            pltpu.VMEM((D_FF, D_MODEL), BF16), pltpu.SemaphoreType.DMA((2,))]),
        out_shape=jax.ShapeDtypeStruct((n_tiles * rows, LANES), F32),
        compiler_params=_cparams(("arbitrary",)),
        name="moe_experts",
    )(*plan, x_sorted, w_gate, w_up, w_down)


def _combine_kernel(dr_ref, ys_hbm, x_ref, wt_ref, o_ref, g_s, sem, *, tm):
    i = pl.program_id(0)
    nt = pl.num_programs(0)
    slot = i % 2
    unroll = 8

    def issue(tile, dst_slot):
        def body(b, c):
            for u in range(unroll):
                r = b * unroll + u
                for k in range(TOP_K):
                    row = dr_ref[(tile * tm + r) * TOP_K + k]
                    pltpu.make_async_copy(ys_hbm.at[pl.ds(row * ROW_TILES, ROW_TILES), :],
                                          g_s.at[dst_slot, pl.ds((k * tm + r) * ROW_TILES, ROW_TILES), :],
                                          sem.at[dst_slot]).start(priority=k)
            return c
        lax.fori_loop(0, tm // unroll, body, 0)

    @pl.when(i == 0)
    def _():
        issue(0, 0)

    @pl.when(i + 1 < nt)
    def _():
        issue(i + 1, 1 - slot)

    pltpu.make_async_copy(ys_hbm.at[pl.ds(0, TOP_K * tm * ROW_TILES), :], g_s.at[slot], sem.at[slot]).wait()
    wt = wt_ref[...]
    y0 = _rows_to_tile(g_s.at[slot], 0, tm, ROW_TILES)
    y1 = _rows_to_tile(g_s.at[slot], tm * ROW_TILES, tm, ROW_TILES)
    o_ref[...] = x_ref[...] + (wt[:, 0:1] * y0 + wt[:, 1:2] * y1)


def _combine(dest_row, y_sorted, x1, wt, tm):
    t = x1.shape[0]
    tile = lambda i, dr: (i, 0)
    return pl.pallas_call(
        functools.partial(_combine_kernel, tm=tm),
        grid_spec=pltpu.PrefetchScalarGridSpec(
            num_scalar_prefetch=1,
            grid=(t // tm,),
            in_specs=[pl.BlockSpec(memory_space=pl.ANY), pl.BlockSpec((tm, D_MODEL), tile),
                      pl.BlockSpec((tm, LANES), tile)],
            out_specs=pl.BlockSpec((tm, D_MODEL), tile),
            scratch_shapes=[pltpu.VMEM((2, TOP_K * tm * ROW_TILES, LANES), F32), pltpu.SemaphoreType.DMA((2,))]),
        out_shape=jax.ShapeDtypeStruct((t, D_MODEL), F32),
        compiler_params=_cparams(("arbitrary",)),
        name="moe_combine",
    )(dest_row, y_sorted, x1, wt)


def _rotary_tables(pos):
    inv_freq = ROPE_THETA ** (-jnp.arange(0, ROT_DIM, 2, dtype=F32) / ROT_DIM)
    ang = pos[:, None] * inv_freq[None, :]
    cos, sin = jnp.cos(ang), jnp.sin(ang)
    rest = LANES - ROT_DIM
    ones = jnp.ones((pos.shape[0], rest), F32)
    zeros = jnp.zeros((pos.shape[0], rest), F32)
    return jnp.concatenate([cos, cos, ones], axis=1), jnp.concatenate([-sin, sin, zeros], axis=1)


def _projections(x, pos, norm1_g, w_in, q_norm_g, k_norm_g):
    t = x.shape[0]
    tm = min(TM_MAX, t)
    h = _rmsnorm(x, norm1_g, tm)
    tcos, tsin = _rotary_tables(pos)
    c_q = 2 * D_CONV
    c_k = c_q + QK_WIDTH
    c_v = c_k + QK_WIDTH
    c_g = c_v + V_WIDTH
    group = jnp.arange(TN) // HEAD_DIM
    group_ones = (group[:, None] == group[None, :]).astype(BF16)
    rot_in = lambda g: [g.reshape(1, LANES), tcos, tsin, group_ones]
    rot_specs = lambda order: [pl.BlockSpec((1, LANES), lambda a, b: (0, 0)),
                               pl.BlockSpec((tm, LANES), lambda a, b: ((a, b)[order], 0)),
                               pl.BlockSpec((tm, LANES), lambda a, b: ((a, b)[order], 0)),
                               pl.BlockSpec((TN, TN), lambda a, b: (0, 0))]
    u_lin = _glu_call(h, w_in, tm)
    q_b = _stationary_call(_q_kernel, h, w_in, c_q, QK_WIDTH, rot_in(q_norm_g), rot_specs(1), BF16, tm, TN, "proj_q")
    k_lin, k_b = _row_linear_call(_k_kernel, h, w_in, c_k, QK_WIDTH, rot_in(k_norm_g), rot_specs(0), tm, "proj_k")
    v_lin, v_b = _row_linear_call(_v_kernel, h, w_in, c_v, V_WIDTH, [], [], tm, "proj_v")
    gates = _stationary_call(_gate_kernel, h, w_in, c_g, 2 * D_MODEL, [], [], BF16, tm, TN_WIDE, "proj_gates")
    return u_lin, q_b, k_lin, k_b, v_lin, v_b, gates


def _layer(x_p, x_s, cache_k, cache_v, state_conv, page_table, lambda_init,
           norm1_g, w_in, conv_w, conv_b, conv_ln_g, conv_ln_b, w_conv_out, q_norm_g, k_norm_g,
           lambda_q1, lambda_k1, lambda_q2, lambda_k2, subln_g, w_attn_out, w_out, norm2_g,
           w_router_group, b_router_group, w_router_expert, b_router_expert, w_gate_e, w_up_e, w_down_e):
    batch, seq, _ = x_p.shape
    nb, n_new, _ = x_s.shape
    past = page_table.shape[1] * PAGE
    tp, ts = batch * seq, nb * n_new
    xp, xs = x_p.reshape(tp, D_MODEL), x_s.reshape(ts, D_MODEL)
    pos_p = jnp.tile(jnp.arange(seq, dtype=F32), batch)
    pos_s = jnp.tile(past + jnp.arange(n_new, dtype=F32), nb)
    w_in = w_in.astype(BF16)
    u_p, q_p, kl_p, kb_p, vl_p, vb_p, gates_p = _projections(xp, pos_p, norm1_g, w_in, q_norm_g, k_norm_g)
    u_s, q_s, kl_s, kb_s, vl_s, vb_s, gates_s = _projections(xs, pos_s, norm1_g, w_in, q_norm_g, k_norm_g)

    u_p = u_p.reshape(batch, seq, CONV_TILES, LANES)
    u_s = u_s.reshape(nb, n_new, CONV_TILES, LANES)
    hist_p = jnp.zeros((batch, HIST, CONV_TILES, LANES), F32)
    hist_s = state_conv.reshape(nb, HIST, CONV_TILES, LANES)
    conv = lambda u, hist, seqs, name: _conv_branch(u, hist, conv_w, conv_b, conv_ln_g, conv_ln_b, seqs, name)
    yc_p = conv(u_p, hist_p, 1, "conv_prompt").reshape(tp * CONV_TILES, LANES)
    yc_s = conv(u_s, hist_s, nb, "conv_sample").reshape(ts * CONV_TILES, LANES)
    carry = lambda hist, u: jnp.concatenate([hist, u], axis=1)[:, u.shape[1]:].reshape(-1, HIST, D_CONV)

    lam_rows = jnp.stack([lambda_q1, lambda_k1, lambda_q2, lambda_k2]).astype(F32)
    o_p = _prompt_attention(q_p, kb_p, vb_p, lam_rows, subln_g, batch, seq, lambda_init)
    pad_rows = lambda a: jnp.pad(a.reshape(nb, n_new, -1), ((0, 0), (0, NEW_ROWS - n_new), (0, 0)))
    o_s = _sample_attention(_query_rows(q_s.reshape(nb, n_new, QK_WIDTH)), pad_rows(kb_s), pad_rows(vb_s),
                            cache_k, cache_v, page_table, lam_rows, subln_g, lambda_init, n_new)
    o_s = o_s[:, :n_new].reshape(ts, V_WIDTH).astype(BF16)

    w_r = jnp.zeros((D_MODEL, LANES), F32).at[:, :N_GROUPS].set(w_router_group)
    w_r = w_r.at[:, N_GROUPS:N_GROUPS + N_EXPERTS].set(w_router_expert)
    w_hi = w_r.astype(BF16)
    w_lo = (w_r - w_hi.astype(F32)).astype(BF16)
    w_split = jnp.concatenate([w_hi, w_lo], axis=1)
    b_r = jnp.zeros((1, LANES), F32).at[0, :N_GROUPS].set(b_router_group)
    b_r = b_r.at[0, N_GROUPS:N_GROUPS + N_EXPERTS].set(b_router_expert)

    def tail(x, yc, o_n, gates):
        tm = min(TM_MAX, x.shape[0])
        m = _merge(yc, o_n, w_conv_out, w_attn_out, gates, min(512, tm))
        x1 = _out_projection(m, w_out, x, tm)
        return (x1,) + tuple(_router(x1, norm2_g, w_split, b_r, min(512, x.shape[0])))

    x1_p, hl_p, e_p, wt_p = tail(xp, yc_p, o_p, gates_p)
    x1_s, hl_s, e_s, wt_s = tail(xs, yc_s, o_s, gates_s)

    n_pairs = (tp + ts) * TOP_K
    n_tiles = -(-n_pairs // MOE_TM) + N_EXPERTS
    e_all = jnp.concatenate([e_p[:, :TOP_K], e_s[:, :TOP_K]], axis=0)
    dest_row, moe_plan = _routing_plan(e_all, n_tiles)
    x_sorted = _dispatch(dest_row, moe_plan[1], hl_p, hl_s, n_tiles)
    y_sorted = _moe(x_sorted, moe_plan, w_gate_e, w_up_e, w_down_e, n_tiles)
    y_p = _combine(dest_row[:tp * TOP_K], y_sorted, x1_p, wt_p, CMB_TM)
    y_s = _combine(dest_row[tp * TOP_K:], y_sorted, x1_s, wt_s, min(CMB_TM, ts))

    v_out = lambda a, lead: a.reshape(lead + (2, N_HEADS, LANES)).swapaxes(-3, -2).reshape(lead + (N_HEADS, V_DIM))
    return (y_p.reshape(batch, seq, D_MODEL), y_s.reshape(nb, n_new, D_MODEL),
            kl_p.reshape(batch, seq, N_HEADS, 2, HEAD_DIM), v_out(vl_p, (batch, seq)),
            carry(hist_p, u_p),
            kl_s.reshape(nb, n_new, N_HEADS, 2, HEAD_DIM), v_out(vl_s, (nb, n_new)),
            carry(hist_s, u_s))


def kernel(x_prompt, x_sample, cache_k, cache_v, state_conv, page_table, norm1_g, w_in, conv_w, conv_b, conv_ln_g, conv_ln_b, w_conv_out, q_norm_g, k_norm_g, lambda_q1, lambda_k1, lambda_q2, lambda_k2, subln_g, w_attn_out, w_out, norm2_g, w_router_group, b_router_group, w_router_expert, b_router_expert, w_gate_e, w_up_e, w_down_e):
    depth = norm1_g.shape[0]
    x_p, x_s = x_prompt, x_sample
    outs = [[] for _ in range(6)]
    for l in range(depth):
        lambda_init = 0.8 - 0.6 * math.exp(-0.3 * l)
        x_p, x_s, *rest = _layer(
            x_p, x_s, cache_k[l], cache_v[l], state_conv[l], page_table, lambda_init,
            norm1_g[l], w_in[l], conv_w[l], conv_b[l], conv_ln_g[l], conv_ln_b[l], w_conv_out[l],
            q_norm_g[l], k_norm_g[l], lambda_q1[l], lambda_k1[l], lambda_q2[l], lambda_k2[l], subln_g[l],
            w_attn_out[l], w_out[l], norm2_g[l], w_router_group[l], b_router_group[l],
            w_router_expert[l], b_router_expert[l], w_gate_e[l], w_up_e[l], w_down_e[l])
        for acc, val in zip(outs, rest):
            acc.append(val)
    return (x_p, x_s) + tuple(jnp.stack(o) for o in outs)
```

```python
import functools
import math

import jax
import jax.numpy as jnp
from jax import lax
from jax.experimental import pallas as pl
from jax.experimental.pallas import tpu as pltpu

F32 = jnp.float32
BF16 = jnp.bfloat16

D_MODEL = 2048
N_HEADS = 8
HEAD_DIM = 128
V_DIM = 2 * HEAD_DIM
QK_WIDTH = N_HEADS * 2 * HEAD_DIM
V_WIDTH = N_HEADS * V_DIM
ROT_DIM = HEAD_DIM // 4
ROPE_THETA = 500000.0
ATTN_SCALE = HEAD_DIM ** -0.5
LOG2E = math.log2(math.e)
NEG_INF = -1e30
D_CONV = D_MODEL // 2
CONV_WIDTH = 31
HIST = CONV_WIDTH - 1
N_GROUPS = 4
EXPERTS_PER_GROUP = 8
N_EXPERTS = N_GROUPS * EXPERTS_PER_GROUP
TOP_K = 2
D_FF = D_MODEL // 4
EPS = 1e-6
PAGE = 128

LANES = 128
SUBLANES = 8
ROW_TILES = D_MODEL // LANES
CONV_TILES = D_CONV // LANES
VMEM_LIMIT = 56 * 1024 * 1024

TM_MAX = 1024
TN = 512
TN_WIDE = 1024
ATTN_ROWS = 256
PAGES_PER_STEP = 8
QROWS = 2 * 4 * SUBLANES
NEW_ROWS = 16
MOE_TM = 256
CMB_TM = 256
CONV_ROWS = 32


def _cparams(sem):
    return pltpu.CompilerParams(dimension_semantics=sem, vmem_limit_bytes=VMEM_LIMIT)


def _rows_to_tile(buf, base, rows, tiles):
    return jnp.concatenate([buf[pl.ds(base + c, rows, stride=tiles), :] for c in range(tiles)], axis=1)


def _rmsnorm_kernel(x_ref, g_ref, o_ref):
    x = x_ref[...]
    ms = jnp.mean(x * x, axis=-1, keepdims=True)
    o_ref[...] = ((x * lax.rsqrt(ms + EPS)) * g_ref[...]).astype(o_ref.dtype)


def _rmsnorm(x, g, tm):
    t, d = x.shape
    return pl.pallas_call(
        _rmsnorm_kernel,
        grid=(t // tm,),
        in_specs=[pl.BlockSpec((tm, d), lambda i: (i, 0)), pl.BlockSpec((1, d), lambda i: (0, 0))],
        out_specs=pl.BlockSpec((tm, d), lambda i: (i, 0)),
        out_shape=jax.ShapeDtypeStruct((t, d), BF16),
        compiler_params=_cparams(("arbitrary",)),
        name="rmsnorm1",
    )(x, g.reshape(1, d))


def _cache_weight(w_ref, w_s):
    @pl.when(pl.program_id(1) == 0)
    def _():
        w_s[...] = w_ref[...].astype(BF16)


def _store_row_linear(o_ref, val, first_tile, tiles, row_of_tile=None):
    rows = val.shape[0]
    for c in range(val.shape[1] // LANES):
        tile = first_tile + c
        dst = tile if row_of_tile is None else row_of_tile(tile)
        o_ref[pl.ds(dst, rows, stride=tiles), :] = val[:, c * LANES:(c + 1) * LANES]


def _glu_kernel(h_ref, wa_ref, wb_ref, u_ref):
    h = h_ref[...]
    a = jnp.dot(h, wa_ref[...], preferred_element_type=F32)
    b = jnp.dot(h, wb_ref[...], preferred_element_type=F32)
    _store_row_linear(u_ref, a * jax.nn.sigmoid(b), pl.program_id(1) * (TN // LANES), CONV_TILES)


def _glu_call(h, w, tm):
    t, k = h.shape
    nj = D_CONV // TN
    return pl.pallas_call(
        _glu_kernel,
        grid=(t // tm, nj),
        in_specs=[pl.BlockSpec((tm, k), lambda i, j: (i, 0)), pl.BlockSpec((k, TN), lambda i, j: (0, j)),
                  pl.BlockSpec((k, TN), lambda i, j: (0, nj + j))],
        out_specs=pl.BlockSpec((tm * CONV_TILES, LANES), lambda i, j: (i, 0)),
        out_shape=jax.ShapeDtypeStruct((t * CONV_TILES, LANES), F32),
        compiler_params=_cparams(("arbitrary", "arbitrary")),
        name="proj_glu",
    )(h, w, w)


def _head_norm_rotary(acc, ones_ref, gain, tcos, tsin):
    ssq = jnp.dot((acc * acc).astype(BF16), ones_ref[...], preferred_element_type=F32)
    tn = acc * lax.rsqrt(ssq * (1.0 / HEAD_DIM) + EPS)
    lane = lax.broadcasted_iota(jnp.int32, (acc.shape[0], LANES), 1)
    half = ROT_DIM // 2
    outs = []
    for g in range(acc.shape[1] // LANES):
        t = tn[:, g * LANES:(g + 1) * LANES] * gain
        partner = jnp.where(lane < half, pltpu.roll(t, LANES - half, 1), pltpu.roll(t, half, 1))
        outs.append(t * tcos + partner * tsin)
    return jnp.concatenate(outs, axis=1)


def _q_kernel(h_ref, w_ref, g_ref, tc_ref, ts_ref, ones_ref, ob_ref):
    acc = jnp.dot(h_ref[...], w_ref[...], preferred_element_type=F32)
    y = _head_norm_rotary(acc, ones_ref, g_ref[...], tc_ref[...], ts_ref[...])
    ob_ref[...] = (y * (ATTN_SCALE * LOG2E)).astype(BF16)


def _gate_kernel(h_ref, w_ref, o_ref):
    o_ref[...] = jax.nn.sigmoid(jnp.dot(h_ref[...], w_ref[...], preferred_element_type=F32)).astype(o_ref.dtype)


def _stationary_call(kernel, h, w, col0, width, extra_in, extra_specs, out_dtype, tm, tn, name):
    t, k = h.shape
    return pl.pallas_call(
        kernel,
        grid=(width // tn, t // tm),
        in_specs=[pl.BlockSpec((tm, k), lambda j, i: (i, 0)),
                  pl.BlockSpec((k, tn), lambda j, i: (0, col0 // tn + j))] + extra_specs,
        out_specs=pl.BlockSpec((tm, tn), lambda j, i: (i, j)),
        out_shape=jax.ShapeDtypeStruct((t, width), out_dtype),
        compiler_params=_cparams(("arbitrary", "arbitrary")),
        name=name,
    )(h, w, *extra_in)


def _k_kernel(h_ref, w_ref, g_ref, tc_ref, ts_ref, ones_ref, of_ref, ob_ref):
    acc = jnp.dot(h_ref[...], w_ref[...], preferred_element_type=F32)
    y = _head_norm_rotary(acc, ones_ref, g_ref[...], tc_ref[...], ts_ref[...])
    _store_row_linear(of_ref, y, pl.program_id(1) * (TN // LANES), ROW_TILES)
    ob_ref[...] = y.astype(BF16)


def _v_kernel(h_ref, w_ref, of_ref, ob_ref):
    acc = jnp.dot(h_ref[...], w_ref[...], preferred_element_type=F32)
    _store_row_linear(of_ref, acc, pl.program_id(1) * (TN // LANES), ROW_TILES,
                      row_of_tile=lambda tile: (tile % 2) * N_HEADS + tile // 2)
    ob_ref[...] = acc.astype(BF16)


def _row_linear_call(kernel, h, w, col0, width, extra_in, extra_specs, tm, name):
    t, k = h.shape
    tiles = width // LANES
    return pl.pallas_call(
        kernel,
        grid=(t // tm, width // TN),
        in_specs=[pl.BlockSpec((tm, k), lambda i, j: (i, 0)),
                  pl.BlockSpec((k, TN), lambda i, j: (0, col0 // TN + j))] + extra_specs,
        out_specs=[pl.BlockSpec((tm * tiles, LANES), lambda i, j: (i, 0)), pl.BlockSpec((tm, TN), lambda i, j: (i, j))],
        out_shape=[jax.ShapeDtypeStruct((t * tiles, LANES), F32), jax.ShapeDtypeStruct((t, width), BF16)],
        compiler_params=_cparams(("arbitrary", "arbitrary")),
        name=name,
    )(h, w, *extra_in)


def _conv_kernel(u_ref, hist_ref, w_ref, b_ref, g_ref, beta_ref, o_ref, win_s, *, length, seqs):
    rows = min(CONV_ROWS, length)
    trips = length // rows

    def finish(acc, s, t0):
        y = acc + b_ref[...]
        mu = jnp.mean(y, axis=(1, 2), keepdims=True)
        yc = y - mu
        var = jnp.mean(yc * yc, axis=(1, 2), keepdims=True)
        z = yc * lax.rsqrt(var + EPS) * g_ref[...] + beta_ref[...]
        o_ref[s, pl.ds(t0, rows)] = z * jax.nn.sigmoid(z)

    def per_sequence(s, carry):
        win_s[0:HIST] = hist_ref[s]
        win_s[HIST:HIST + rows] = u_ref[s, 0:rows]
        acc = jnp.zeros((rows, CONV_TILES, LANES), F32)
        for j in range(CONV_WIDTH):
            acc = acc + win_s[j:j + rows] * w_ref[j]
        finish(acc, s, 0)

        def step(i, c):
            t0 = i * rows
            acc = jnp.zeros((rows, CONV_TILES, LANES), F32)
            for j in range(CONV_WIDTH):
                acc = acc + u_ref[s, pl.ds(t0 - HIST + j, rows)] * w_ref[j]
            finish(acc, s, t0)
            return c

        if trips > 1:
            rest = trips - 1
            lax.fori_loop(1, trips, step, 0, unroll=next(u for u in (2, 3, 1) if rest % u == 0))
        return carry

    if seqs == 1:
        per_sequence(0, 0)
    else:
        lax.fori_loop(0, seqs, per_sequence, 0)


def _conv_branch(u, hist, conv_w, conv_b, ln_g, ln_b, seqs, name):
    b, length = u.shape[:2]
    assert length <= CONV_ROWS or CONV_ROWS >= HIST, "blocks after the first must start past the carried history"
    tile = lambda a: a.reshape(-1, CONV_TILES, LANES)
    whole = lambda n: pl.BlockSpec((n, CONV_TILES, LANES), lambda i: (0, 0, 0))
    per_step = lambda n: pl.BlockSpec((seqs, n, CONV_TILES, LANES), lambda i: (i, 0, 0, 0))
    return pl.pallas_call(
        functools.partial(_conv_kernel, length=length, seqs=seqs),
        grid=(b // seqs,),
        in_specs=[per_step(length), per_step(HIST), whole(CONV_WIDTH), whole(1), whole(1), whole(1)],
        out_specs=per_step(length),
        out_shape=jax.ShapeDtypeStruct((b, length, CONV_TILES, LANES), F32),
        scratch_shapes=[pltpu.VMEM((HIST + min(CONV_ROWS, length), CONV_TILES, LANES), F32)],
        compiler_params=_cparams(("arbitrary",)),
        name=name,
    )(u, hist, tile(conv_w), tile(conv_b), tile(ln_g), tile(ln_b))


def _lambda_value(lam_ref, lambda_init):
    v = lam_ref[...]
    s1 = jnp.sum(v[0:1] * v[1:2], axis=-1, keepdims=True)
    s2 = jnp.sum(v[2:3] * v[3:4], axis=-1, keepdims=True)
    return jnp.exp(s1) - jnp.exp(s2) + lambda_init


def _prompt_attn_kernel(q_ref, k_ref, v_ref, lam_ref, sg_ref, o_ref, *, lambda_init, seq):
    rb = ATTN_ROWS
    lam = _lambda_value(lam_ref, lambda_init)
    row = lax.broadcasted_iota(jnp.int32, (rb, rb), 0)
    col = lax.broadcasted_iota(jnp.int32, (rb, rb), 1)
    for r in range(seq // rb):
        rows = pl.ds(r * rb, rb)
        kc = (r + 1) * rb
        v = v_ref[0:kc, :]
        outs = []
        for c in range(2):
            q = q_ref[rows, c * HEAD_DIM:(c + 1) * HEAD_DIM]
            k = k_ref[0:kc, c * HEAD_DIM:(c + 1) * HEAD_DIM]
            s = lax.dot_general(q, k, (((1,), (1,)), ((), ())), preferred_element_type=F32)
            diag = jnp.where(col <= row, s[:, kc - rb:], NEG_INF)
            s = diag if r == 0 else jnp.concatenate([s[:, :kc - rb], diag], axis=1)
            p = jnp.exp2(s - jnp.max(s, axis=-1, keepdims=True))
            l = jnp.sum(p, axis=-1, keepdims=True)
            outs.append(jnp.dot(p.astype(BF16), v, preferred_element_type=F32) / l)
        o = outs[0] - lam * outs[1]
        ms = jnp.mean(o * o, axis=-1, keepdims=True)
        o_ref[rows, :] = (((o * lax.rsqrt(ms + EPS)) * sg_ref[...]) * (1.0 - lambda_init)).astype(o_ref.dtype)


def _prompt_attention(q, k, v, lam_rows, subln_g, batch, seq, lambda_init):
    blk = lambda b, h: (b, h)
    return pl.pallas_call(
        functools.partial(_prompt_attn_kernel, lambda_init=lambda_init, seq=seq),
        grid=(batch, N_HEADS),
        in_specs=[pl.BlockSpec((seq, V_DIM), blk), pl.BlockSpec((seq, V_DIM), blk), pl.BlockSpec((seq, V_DIM), blk),
                  pl.BlockSpec((4, HEAD_DIM), lambda b, h: (0, 0)), pl.BlockSpec((1, V_DIM), lambda b, h: (0, 0))],
        out_specs=pl.BlockSpec((seq, V_DIM), blk),
        out_shape=jax.ShapeDtypeStruct((batch * seq, V_WIDTH), BF16),
        compiler_params=_cparams(("arbitrary", "arbitrary")),
        name="prompt_attention",
    )(q, k, v, lam_rows, subln_g.reshape(1, V_DIM))


def _page_tiles(x):
    tiles = jnp.swapaxes(x, 0, 1)
    return [tiles[i].astype(BF16) for i in range(SUBLANES)]


def _sample_attn_kernel(pt_ref, *refs, n_new, lambda_init):
    g = PAGES_PER_STEP
    k_refs, v_refs = refs[:g], refs[g:2 * g]
    q_ref, kn_ref, vn_ref, lam_ref, sg_ref, o_ref, m_s, l_s, acc_s = refs[2 * g:]
    j = pl.program_id(1)
    q2 = q_ref[0]
    nt_dims = (((1,), (1,)), ((), ()))

    @pl.when(j == 0)
    def _():
        m_s[...] = jnp.full(m_s.shape, NEG_INF, F32)
        l_s[...] = jnp.zeros(l_s.shape, F32)
        acc_s[...] = jnp.zeros(acc_s.shape, F32)

    def online_update(s, values):
        m_old = m_s[...]
        m_new = jnp.maximum(m_old, jnp.max(s, axis=-1, keepdims=True))
        alpha = jnp.exp2(m_old - m_new)
        p = jnp.exp2(s - m_new)
        l_s[...] = alpha * l_s[...] + jnp.sum(p, axis=-1, keepdims=True)
        pv = None
        c0 = 0
        for n, v in values:
            part = jnp.dot(p[:, c0:c0 + n].astype(BF16), v, preferred_element_type=F32)
            pv = part if pv is None else pv + part
            c0 += n
        acc_s[...] = alpha * acc_s[...] + pv
        m_s[...] = m_new

    s_parts, values = [], []
    for p in range(g):
        kfull = jnp.concatenate(_page_tiles(k_refs[p][:, 0]) + _page_tiles(k_refs[p][:, 1]), axis=1)
        s_parts.append(lax.dot_general(q2, kfull, nt_dims, preferred_element_type=F32))
        t0, t1 = _page_tiles(v_refs[p][:, :, :LANES]), _page_tiles(v_refs[p][:, :, LANES:])
        values.append((PAGE, jnp.concatenate([t for pair in zip(t0, t1) for t in pair], axis=1)))
    online_update(jnp.concatenate(s_parts, axis=1), values)

    @pl.when(j == pl.num_programs(1) - 1)
    def _():
        kn = kn_ref[0]
        s = lax.dot_general(q2, kn, nt_dims, preferred_element_type=F32)
        key = lax.broadcasted_iota(jnp.int32, s.shape, 1)
        qry = (lax.broadcasted_iota(jnp.int32, s.shape, 0) % (4 * SUBLANES)) // SUBLANES
        s = jnp.where((key <= qry) & (key < n_new), s, NEG_INF)
        online_update(s, [(kn.shape[0], vn_ref[0])])
        lam = _lambda_value(lam_ref, lambda_init)
        half = QROWS // 2
        acc, l = acc_s[...], l_s[...]
        o_all = acc[:half] / l[:half] - lam * (acc[half:] / l[half:])
        head = lax.broadcasted_iota(jnp.int32, (SUBLANES, V_WIDTH), 0)
        own = head == lax.broadcasted_iota(jnp.int32, (SUBLANES, V_WIDTH), 1) // V_DIM
        rows = []
        for q in range(n_new):
            o = jnp.where(own, o_all[q * SUBLANES:(q + 1) * SUBLANES, :], 0.0)
            ms = jnp.sum(o * o, axis=-1, keepdims=True) / V_DIM
            o = jnp.sum(o * lax.rsqrt(ms + EPS), axis=0, keepdims=True)
            rows.append((o * sg_ref[...]) * (1.0 - lambda_init))
        rows.append(jnp.zeros((SUBLANES - n_new, V_WIDTH), F32))
        o_ref[0] = jnp.concatenate(rows, axis=0)


def _sample_attention(q_rows, k_new, v_new, cache_k, cache_v, page_table, lam_rows, subln_g, lambda_init, n_new):
    nb, n_pages = page_table.shape
    g = PAGES_PER_STEP
    n_pool = cache_k.shape[0]
    ck = cache_k.reshape(n_pool * PAGE, 2, SUBLANES, HEAD_DIM)
    cv = cache_v.reshape(n_pool * PAGE, N_HEADS, V_DIM)
    pt = page_table.reshape(-1)
    kmap = lambda p: (lambda b, j, pt: (pt[b * n_pages + j * g + p], 0, 0, 0))
    vmap = lambda p: (lambda b, j, pt: (pt[b * n_pages + j * g + p], 0, 0))
    k_specs = [pl.BlockSpec((PAGE, 2, SUBLANES, HEAD_DIM), kmap(p)) for p in range(g)]
    v_specs = [pl.BlockSpec((PAGE, N_HEADS, V_DIM), vmap(p)) for p in range(g)]
    new_rows = k_new.shape[1]
    per_b = lambda shape: pl.BlockSpec((1,) + shape, lambda b, j, pt: (b, 0, 0))
    const = lambda shape: pl.BlockSpec(shape, lambda b, j, pt: (0, 0))
    tiled_g = jnp.tile(subln_g.reshape(1, V_DIM), (1, N_HEADS))
    return pl.pallas_call(
        functools.partial(_sample_attn_kernel, n_new=n_new, lambda_init=lambda_init),
        grid_spec=pltpu.PrefetchScalarGridSpec(
            num_scalar_prefetch=1,
            grid=(nb, n_pages // g),
            in_specs=k_specs + v_specs + [per_b((QROWS, QK_WIDTH)), per_b((new_rows, QK_WIDTH)),
                                          per_b((new_rows, V_WIDTH)), const((4, HEAD_DIM)), const((1, V_WIDTH))],
            out_specs=per_b((SUBLANES, V_WIDTH)),
            scratch_shapes=[pltpu.VMEM((QROWS, 1), F32), pltpu.VMEM((QROWS, 1), F32), pltpu.VMEM((QROWS, V_WIDTH), F32)]),
        out_shape=jax.ShapeDtypeStruct((nb, SUBLANES, V_WIDTH), F32),
        compiler_params=_cparams(("arbitrary", "arbitrary")),
        name="sample_attention",
    )(pt, *([ck] * g), *([cv] * g), q_rows, k_new, v_new, lam_rows, tiled_g)


def _query_rows(q_s):
    n_new = q_s.shape[1]
    r = jnp.arange(QROWS)
    r_c, r_q, r_h = r // (4 * SUBLANES), (r % (4 * SUBLANES)) // SUBLANES, r % SUBLANES
    col_hc = jnp.arange(QK_WIDTH) // HEAD_DIM
    keep = (col_hc[None, :] == (r_h * 2 + r_c)[:, None]) & (r_q < n_new)[:, None]
    picked = q_s[:, jnp.minimum(r_q, n_new - 1), :]
    return jnp.where(keep[None], picked, jnp.zeros((), q_s.dtype))


def _merge_kernel(yc_ref, ya_ref, wc_ref, wa_ref, gc_ref, ga_ref, m_ref, wc_s, wa_s):
    _cache_weight(wc_ref, wc_s)
    _cache_weight(wa_ref, wa_s)
    conv = _rows_to_tile(yc_ref, 0, ya_ref.shape[0], CONV_TILES).astype(BF16)
    yc = jnp.dot(conv, wc_s[...], preferred_element_type=F32)
    ya = jnp.dot(ya_ref[...], wa_s[...], preferred_element_type=F32)
    m_ref[...] = (gc_ref[...] * yc + ga_ref[...] * ya).astype(m_ref.dtype)


def _merge(yconv_lin, o_n, w_conv_out, w_attn_out, gates, tm):
    t = o_n.shape[0]
    tn = TN_WIDE
    nj, ni = D_MODEL // tn, t // tm
    return pl.pallas_call(
        _merge_kernel,
        grid=(nj, ni),
        in_specs=[pl.BlockSpec((tm * CONV_TILES, LANES), lambda j, i: (i, 0)),
                  pl.BlockSpec((tm, V_WIDTH), lambda j, i: (i, 0)),
                  pl.BlockSpec((D_CONV, tn), lambda j, i: (0, j)), pl.BlockSpec((V_WIDTH, tn), lambda j, i: (0, j)),
                  pl.BlockSpec((tm, tn), lambda j, i: (i, j)), pl.BlockSpec((tm, tn), lambda j, i: (i, nj + j))],
        out_specs=pl.BlockSpec((tm, tn), lambda j, i: (i, j)),
        out_shape=jax.ShapeDtypeStruct((t, D_MODEL), BF16),
        scratch_shapes=[pltpu.VMEM((D_CONV, tn), BF16), pltpu.VMEM((V_WIDTH, tn), BF16)],
        compiler_params=_cparams(("arbitrary", "arbitrary")),
        name="merge",
    )(yconv_lin, o_n, w_conv_out, w_attn_out, gates, gates)


def _residual_kernel(m_ref, w_ref, x_ref, o_ref, w_s):
    _cache_weight(w_ref, w_s)
    o_ref[...] = x_ref[...] + jnp.dot(m_ref[...], w_s[...], preferred_element_type=F32)


def _out_projection(m, w_out, x, tm):
    t = m.shape[0]
    tn = TN_WIDE
    return pl.pallas_call(
        _residual_kernel,
        grid=(D_MODEL // tn, t // tm),
        in_specs=[pl.BlockSpec((tm, D_MODEL), lambda j, i: (i, 0)), pl.BlockSpec((D_MODEL, tn), lambda j, i: (0, j)),
                  pl.BlockSpec((tm, tn), lambda j, i: (i, j))],
        out_specs=pl.BlockSpec((tm, tn), lambda j, i: (i, j)),
        out_shape=jax.ShapeDtypeStruct((t, D_MODEL), F32),
        scratch_shapes=[pltpu.VMEM((D_MODEL, tn), BF16)],
        compiler_params=_cparams(("arbitrary", "arbitrary")),
        name="out_projection",
    )(m, w_out, x)


def _router_kernel(x_ref, g_ref, w_ref, b_ref, hl_ref, e_ref, wt_ref):
    x = x_ref[...]
    ms = jnp.mean(x * x, axis=-1, keepdims=True)
    h = (x * lax.rsqrt(ms + EPS)) * g_ref[...]
    _store_row_linear(hl_ref, h, 0, ROW_TILES)
    h_hi = h.astype(BF16)
    h_lo = (h - h_hi.astype(F32)).astype(BF16)
    both = jnp.dot(h_hi, w_ref[...], preferred_element_type=F32)
    logits = both[:, :LANES] + both[:, LANES:] + jnp.dot(h_lo, w_ref[:, :LANES], preferred_element_type=F32)
    logits = logits + b_ref[...]
    lane = lax.broadcasted_iota(jnp.int32, logits.shape, 1)
    neg = -jnp.inf
    big = jnp.int32(LANES)
    lg = jnp.where(lane < N_GROUPS, logits, neg)
    eg = jnp.exp(lg - jnp.max(lg, axis=-1, keepdims=True))
    pg = eg / jnp.sum(eg, axis=-1, keepdims=True)
    p_top = jnp.max(pg, axis=-1, keepdims=True)
    g_idx = jnp.min(jnp.where(pg == p_top, lane, big), axis=-1, keepdims=True)
    in_group = (lane >= N_GROUPS) & (lane < N_GROUPS + N_EXPERTS) & (((lane - N_GROUPS) // EXPERTS_PER_GROUP) == g_idx)
    le = jnp.where(in_group, logits, neg)
    ee = jnp.exp(le - jnp.max(le, axis=-1, keepdims=True))
    pe = jnp.where(in_group, ee / jnp.sum(ee, axis=-1, keepdims=True), -1.0)
    w1 = jnp.max(pe, axis=-1, keepdims=True)
    i1 = jnp.min(jnp.where(pe == w1, lane, big), axis=-1, keepdims=True)
    pe2 = jnp.where(lane == i1, -1.0, pe)
    w2 = jnp.max(pe2, axis=-1, keepdims=True)
    i2 = jnp.min(jnp.where(pe2 == w2, lane, big), axis=-1, keepdims=True)
    denom = w1 + w2
    e_ref[...] = jnp.where(lane == 0, i1 - N_GROUPS, jnp.where(lane == 1, i2 - N_GROUPS, 0))
    wt_ref[...] = jnp.where(lane == 0, w1 / denom * p_top, jnp.where(lane == 1, w2 / denom * p_top, 0.0))


def _router(x1, norm2_g, w_router, b_router, tm):
    t = x1.shape[0]
    return pl.pallas_call(
        _router_kernel,
        grid=(t // tm,),
        in_specs=[pl.BlockSpec((tm, D_MODEL), lambda i: (i, 0)), pl.BlockSpec((1, D_MODEL), lambda i: (0, 0)),
                  pl.BlockSpec((D_MODEL, 2 * LANES), lambda i: (0, 0)), pl.BlockSpec((1, LANES), lambda i: (0, 0))],
        out_specs=[pl.BlockSpec((tm * ROW_TILES, LANES), lambda i: (i, 0)),
                   pl.BlockSpec((tm, LANES), lambda i: (i, 0)), pl.BlockSpec((tm, LANES), lambda i: (i, 0))],
        out_shape=[jax.ShapeDtypeStruct((t * ROW_TILES, LANES), F32),
                   jax.ShapeDtypeStruct((t, LANES), jnp.int32), jax.ShapeDtypeStruct((t, LANES), F32)],
        compiler_params=_cparams(("arbitrary",)),
        name="router",
    )(x1, norm2_g.reshape(1, D_MODEL), w_router, b_router)


def _routing_plan(expert_idx, n_tiles):
    e_flat = expert_idx.reshape(-1)
    onehot = (e_flat[:, None] == jnp.arange(N_EXPERTS, dtype=jnp.int32)[None, :]).astype(jnp.int32)
    csum = jnp.cumsum(onehot, axis=0)
    counts = csum[-1]
    rank = jnp.sum(onehot * csum, axis=1) - 1
    tiles_e = (counts + MOE_TM - 1) // MOE_TM
    tile_end = jnp.cumsum(tiles_e)
    tile_start = tile_end - tiles_e
    dest_row = jnp.sum(onehot * tile_start[None, :], axis=1) * MOE_TM + rank
    n_used = tile_end[-1]
    tile_id = jnp.arange(n_tiles, dtype=jnp.int32)
    te = jnp.sum((tile_end[None, :] <= jnp.minimum(tile_id, n_used - 1)[:, None]).astype(jnp.int32), axis=1)
    te = jnp.minimum(te, N_EXPERTS - 1)
    left = counts[te] - (tile_id - tile_start[te]) * MOE_TM
    tile_rows = jnp.where(tile_id < n_used, jnp.clip(left, 0, MOE_TM), 0)
    first = (tile_id < n_used) & (tile_id == tile_start[te])
    slot = (jnp.cumsum(first.astype(jnp.int32)) - 1) % 2
    ids = jnp.arange(N_EXPERTS, dtype=jnp.int32)
    later_used = (ids[None, :] > ids[:, None]) & (counts > 0)[None, :]
    next_used = jnp.min(jnp.where(later_used, ids[None, :], N_EXPERTS), axis=1)
    next_expert = jnp.where(next_used[te] < N_EXPERTS, next_used[te], -1)
    i32 = lambda a: a.astype(jnp.int32)
    moe_plan = (i32(te), i32(tile_rows), i32(n_used).reshape(1), i32(first), i32(slot), i32(next_expert))
    return i32(dest_row), moe_plan


def _dispatch_kernel(dr_ref, tr_ref, hp_ref, hs_ref, xs_hbm, zero_s, sem, zsem, *, tile_p, p_steps, n_sample, n_tiles):
    i = pl.program_id(0)
    unroll = 8

    def scatter(src_ref, n_tok, pair0):
        def body(b, c):
            for u in range(unroll):
                r = b * unroll + u
                for k in range(TOP_K):
                    dst = dr_ref[pair0 + r * TOP_K + k]
                    pltpu.make_async_copy(src_ref.at[pl.ds(r * ROW_TILES, ROW_TILES), :],
                                          xs_hbm.at[pl.ds(dst * ROW_TILES, ROW_TILES), :], sem).start(priority=k)
            return c

        lax.fori_loop(0, n_tok // unroll, body, 0)
        for _ in range(TOP_K):
            pltpu.make_async_copy(src_ref, xs_hbm.at[pl.ds(0, n_tok * ROW_TILES), :], sem).wait()

    @pl.when(i < p_steps)
    def _():
        scatter(hp_ref, tile_p, i * (tile_p * TOP_K))

    @pl.when(i == p_steps)
    def _():
        scatter(hs_ref, n_sample, p_steps * tile_p * TOP_K)
        zero_s[...] = jnp.zeros(zero_s.shape, F32)

        def pad_body(t, carry):
            n = tr_ref[t]

            @pl.when(n < MOE_TM)
            def _():
                rows = (MOE_TM - n) * ROW_TILES
                cp = pltpu.make_async_copy(zero_s.at[pl.ds(0, rows), :],
                                           xs_hbm.at[pl.ds((t * MOE_TM + n) * ROW_TILES, rows), :], zsem)
                cp.start()
                cp.wait()
            return carry

        lax.fori_loop(0, n_tiles, pad_body, 0)


def _dispatch(dest_row, tile_rows, h_lin_p, h_lin_s, n_tiles):
    n_prompt, n_sample = h_lin_p.shape[0] // ROW_TILES, h_lin_s.shape[0] // ROW_TILES
    tile_p = min(512, n_prompt)
    p_steps = n_prompt // tile_p
    return pl.pallas_call(
        functools.partial(_dispatch_kernel, tile_p=tile_p, p_steps=p_steps, n_sample=n_sample, n_tiles=n_tiles),
        grid_spec=pltpu.PrefetchScalarGridSpec(
            num_scalar_prefetch=2,
            grid=(p_steps + 1,),
            in_specs=[pl.BlockSpec((tile_p * ROW_TILES, LANES), lambda i, dr, tr: (jnp.minimum(i, p_steps - 1), 0)),
                      pl.BlockSpec((n_sample * ROW_TILES, LANES), lambda i, dr, tr: (0, 0))],
            out_specs=pl.BlockSpec(memory_space=pl.ANY),
            scratch_shapes=[pltpu.VMEM((MOE_TM * ROW_TILES, LANES), F32), pltpu.SemaphoreType.DMA(()),
                            pltpu.SemaphoreType.DMA(())]),
        out_shape=jax.ShapeDtypeStruct((n_tiles * MOE_TM * ROW_TILES, LANES), F32),
        compiler_params=_cparams(("arbitrary",)),
        name="moe_dispatch",
    )(dest_row, tile_rows, h_lin_p, h_lin_s)


def _moe_kernel(te_ref, tr_ref, nu_ref, first_ref, slot_ref, next_ref, x_ref, wg_hbm, wu_hbm, wd_hbm, y_ref,
                wg_f, wu_f, wd_f, wg_s, wu_s, wd_s, sem):
    i = pl.program_id(0)
    n = tr_ref[i]

    def weight_copies(e, slot):
        return [pltpu.make_async_copy(src.at[e], dst.at[slot], sem.at[slot])
                for src, dst in ((wg_hbm, wg_f), (wu_hbm, wu_f), (wd_hbm, wd_f))]

    @pl.when(i == 0)
    def _():
        for cp in weight_copies(te_ref[0], 0):
            cp.start()

    @pl.when(n == 0)
    def _():
        y_ref[...] = jnp.zeros(y_ref.shape, F32)

    @pl.when(n > 0)
    def _():
        @pl.when(first_ref[i] == 1)
        def _():
            slot = slot_ref[i]
            for cp in weight_copies(te_ref[i], slot):
                cp.wait()
            wg_s[...] = wg_f[slot].astype(BF16)
            wu_s[...] = wu_f[slot].astype(BF16)
            wd_s[...] = wd_f[slot].astype(BF16)

            @pl.when(next_ref[i] >= 0)
            def _():
                for cp in weight_copies(next_ref[i], 1 - slot):
                    cp.start()

        x = _rows_to_tile(x_ref, 0, MOE_TM, ROW_TILES).astype(BF16)
        hg = jnp.dot(x, wg_s[...], preferred_element_type=F32)
        hu = jnp.dot(x, wu_s[...], preferred_element_type=F32)
        act = ((hg * jax.nn.sigmoid(hg)) * hu).astype(BF16)
        y = jnp.dot(act, wd_s[...], preferred_element_type=F32)
        _store_row_linear(y_ref, y, 0, ROW_TILES)


def _moe(x_sorted, plan, w_gate, w_up, w_down, n_tiles):
    rows = MOE_TM * ROW_TILES
    any_spec = pl.BlockSpec(memory_space=pl.ANY)
    return pl.pallas_call(
        _moe_kernel,
        grid_spec=pltpu.PrefetchScalarGridSpec(
            num_scalar_prefetch=len(plan),
            grid=(n_tiles,),
            in_specs=[pl.BlockSpec((rows, LANES), lambda i, te, tr, nu, *_: (jnp.minimum(i, nu[0] - 1), 0)),
                      any_spec, any_spec, any_spec],
            out_specs=pl.BlockSpec((rows, LANES), lambda i, *_: (i, 0)),
            scratch_shapes=[pltpu.VMEM((2, D_MODEL, D_FF), F32), pltpu.VMEM((2, D_MODEL, D_FF), F32),
                            pltpu.VMEM((2, D_FF, D_MODEL), F32),
                            pltpu.VMEM((D_MODEL, D_FF), BF16), pltpu.VMEM((D_MODEL, D_FF), BF16),
                            pltpu.VMEM((D_FF, D_MODEL), BF16), pltpu.SemaphoreType.DMA((2,))]),
        out_shape=jax.ShapeDtypeStruct((n_tiles * rows, LANES), F32),
        compiler_params=_cparams(("arbitrary",)),
        name="moe_experts",
    )(*plan, x_sorted, w_gate, w_up, w_down)


def _combine_kernel(dr_ref, ys_hbm, x_ref, wt_ref, o_ref, g_s, sem, *, tm):
    i = pl.program_id(0)
    nt = pl.num_programs(0)
    slot = i % 2
    unroll = 8

    def issue(tile, dst_slot):
        def body(b, c):
            for u in range(unroll):
                r = b * unroll + u
                for k in range(TOP_K):
                    row = dr_ref[(tile * tm + r) * TOP_K + k]
                    pltpu.make_async_copy(ys_hbm.at[pl.ds(row * ROW_TILES, ROW_TILES), :],
                                          g_s.at[dst_slot, pl.ds((k * tm + r) * ROW_TILES, ROW_TILES), :],
                                          sem.at[dst_slot]).start(priority=k)
            return c
        lax.fori_loop(0, tm // unroll, body, 0)

    @pl.when(i == 0)
    def _():
        issue(0, 0)

    @pl.when(i + 1 < nt)
    def _():
        issue(i + 1, 1 - slot)

    pltpu.make_async_copy(ys_hbm.at[pl.ds(0, TOP_K * tm * ROW_TILES), :], g_s.at[slot], sem.at[slot]).wait()
    wt = wt_ref[...]
    y0 = _rows_to_tile(g_s.at[slot], 0, tm, ROW_TILES)
    y1 = _rows_to_tile(g_s.at[slot], tm * ROW_TILES, tm, ROW_TILES)
    o_ref[...] = x_ref[...] + (wt[:, 0:1] * y0 + wt[:, 1:2] * y1)


def _combine(dest_row, y_sorted, x1, wt, tm):
    t = x1.shape[0]
    tile = lambda i, dr: (i, 0)
    return pl.pallas_call(
        functools.partial(_combine_kernel, tm=tm),
        grid_spec=pltpu.PrefetchScalarGridSpec(
            num_scalar_prefetch=1,
            grid=(t // tm,),
            in_specs=[pl.BlockSpec(memory_space=pl.ANY), pl.BlockSpec((tm, D_MODEL), tile),
                      pl.BlockSpec((tm, LANES), tile)],
            out_specs=pl.BlockSpec((tm, D_MODEL), tile),
            scratch_shapes=[pltpu.VMEM((2, TOP_K * tm * ROW_TILES, LANES), F32), pltpu.SemaphoreType.DMA((2,))]),
        out_shape=jax.ShapeDtypeStruct((t, D_MODEL), F32),
        compiler_params=_cparams(("arbitrary",)),
        name="moe_combine",
    )(dest_row, y_sorted, x1, wt)


def _rotary_tables(pos):
    inv_freq = ROPE_THETA ** (-jnp.arange(0, ROT_DIM, 2, dtype=F32) / ROT_DIM)
    ang = pos[:, None] * inv_freq[None, :]
    cos, sin = jnp.cos(ang), jnp.sin(ang)
    rest = LANES - ROT_DIM
    ones = jnp.ones((pos.shape[0], rest), F32)
    zeros = jnp.zeros((pos.shape[0], rest), F32)
    return jnp.concatenate([cos, cos, ones], axis=1), jnp.concatenate([-sin, sin, zeros], axis=1)


def _projections(x, pos, norm1_g, w_in, q_norm_g, k_norm_g):
    t = x.shape[0]
    tm = min(TM_MAX, t)
    h = _rmsnorm(x, norm1_g, tm)
    tcos, tsin = _rotary_tables(pos)
    c_q = 2 * D_CONV
    c_k = c_q + QK_WIDTH
    c_v = c_k + QK_WIDTH
    c_g = c_v + V_WIDTH
    group = jnp.arange(TN) // HEAD_DIM
    group_ones = (group[:, None] == group[None, :]).astype(BF16)
    rot_in = lambda g: [g.reshape(1, LANES), tcos, tsin, group_ones]
    rot_specs = lambda order: [pl.BlockSpec((1, LANES), lambda a, b: (0, 0)),
                               pl.BlockSpec((tm, LANES), lambda a, b: ((a, b)[order], 0)),
                               pl.BlockSpec((tm, LANES), lambda a, b: ((a, b)[order], 0)),
                               pl.BlockSpec((TN, TN), lambda a, b: (0, 0))]
    u_lin = _glu_call(h, w_in, tm)
    q_b = _stationary_call(_q_kernel, h, w_in, c_q, QK_WIDTH, rot_in(q_norm_g), rot_specs(1), BF16, tm, TN, "proj_q")
    k_lin, k_b = _row_linear_call(_k_kernel, h, w_in, c_k, QK_WIDTH, rot_in(k_norm_g), rot_specs(0), tm, "proj_k")
    v_lin, v_b = _row_linear_call(_v_kernel, h, w_in, c_v, V_WIDTH, [], [], tm, "proj_v")
    gates = _stationary_call(_gate_kernel, h, w_in, c_g, 2 * D_MODEL, [], [], BF16, tm, 2 * TN_WIDE, "proj_gates")
    return u_lin, q_b, k_lin, k_b, v_lin, v_b, gates


def _layer(x_p, x_s, cache_k, cache_v, state_conv, page_table, lambda_init,
           norm1_g, w_in, conv_w, conv_b, conv_ln_g, conv_ln_b, w_conv_out, q_norm_g, k_norm_g,
           lambda_q1, lambda_k1, lambda_q2, lambda_k2, subln_g, w_attn_out, w_out, norm2_g,
           w_router_group, b_router_group, w_router_expert, b_router_expert, w_gate_e, w_up_e, w_down_e):
    batch, seq, _ = x_p.shape
    nb, n_new, _ = x_s.shape
    past = page_table.shape[1] * PAGE
    tp, ts = batch * seq, nb * n_new
    xp, xs = x_p.reshape(tp, D_MODEL), x_s.reshape(ts, D_MODEL)
    pos_p = jnp.tile(jnp.arange(seq, dtype=F32), batch)
    pos_s = jnp.tile(past + jnp.arange(n_new, dtype=F32), nb)
    w_in = w_in.astype(BF16)
    u_p, q_p, kl_p, kb_p, vl_p, vb_p, gates_p = _projections(xp, pos_p, norm1_g, w_in, q_norm_g, k_norm_g)
    u_s, q_s, kl_s, kb_s, vl_s, vb_s, gates_s = _projections(xs, pos_s, norm1_g, w_in, q_norm_g, k_norm_g)

    u_p = u_p.reshape(batch, seq, CONV_TILES, LANES)
    u_s = u_s.reshape(nb, n_new, CONV_TILES, LANES)
    hist_p = jnp.zeros((batch, HIST, CONV_TILES, LANES), F32)
    hist_s = state_conv.reshape(nb, HIST, CONV_TILES, LANES)
    conv = lambda u, hist, seqs, name: _conv_branch(u, hist, conv_w, conv_b, conv_ln_g, conv_ln_b, seqs, name)
    yc_p = conv(u_p, hist_p, 1, "conv_prompt").reshape(tp * CONV_TILES, LANES)
    yc_s = conv(u_s, hist_s, nb, "conv_sample").reshape(ts * CONV_TILES, LANES)
    carry = lambda hist, u: jnp.concatenate([hist, u], axis=1)[:, u.shape[1]:].reshape(-1, HIST, D_CONV)

    lam_rows = jnp.stack([lambda_q1, lambda_k1, lambda_q2, lambda_k2]).astype(F32)
    o_p = _prompt_attention(q_p, kb_p, vb_p, lam_rows, subln_g, batch, seq, lambda_init)
    pad_rows = lambda a: jnp.pad(a.reshape(nb, n_new, -1), ((0, 0), (0, NEW_ROWS - n_new), (0, 0)))
    o_s = _sample_attention(_query_rows(q_s.reshape(nb, n_new, QK_WIDTH)), pad_rows(kb_s), pad_rows(vb_s),
                            cache_k, cache_v, page_table, lam_rows, subln_g, lambda_init, n_new)
    o_s = o_s[:, :n_new].reshape(ts, V_WIDTH).astype(BF16)

    w_r = jnp.zeros((D_MODEL, LANES), F32).at[:, :N_GROUPS].set(w_router_group)
    w_r = w_r.at[:, N_GROUPS:N_GROUPS + N_EXPERTS].set(w_router_expert)
    w_hi = w_r.astype(BF16)
    w_lo = (w_r - w_hi.astype(F32)).astype(BF16)
    w_split = jnp.concatenate([w_hi, w_lo], axis=1)
    b_r = jnp.zeros((1, LANES), F32).at[0, :N_GROUPS].set(b_router_group)
    b_r = b_r.at[0, N_GROUPS:N_GROUPS + N_EXPERTS].set(b_router_expert)

    def tail(x, yc, o_n, gates):
        tm = min(TM_MAX, x.shape[0])
        m = _merge(yc, o_n, w_conv_out, w_attn_out, gates, min(512, tm))
        x1 = _out_projection(m, w_out, x, tm)
        return (x1,) + tuple(_router(x1, norm2_g, w_split, b_r, min(512, x.shape[0])))

    x1_p, hl_p, e_p, wt_p = tail(xp, yc_p, o_p, gates_p)
    x1_s, hl_s, e_s, wt_s = tail(xs, yc_s, o_s, gates_s)

    n_pairs = (tp + ts) * TOP_K
    n_tiles = -(-n_pairs // MOE_TM) + N_EXPERTS
    e_all = jnp.concatenate([e_p[:, :TOP_K], e_s[:, :TOP_K]], axis=0)
    dest_row, moe_plan = _routing_plan(e_all, n_tiles)
    x_sorted = _dispatch(dest_row, moe_plan[1], hl_p, hl_s, n_tiles)
    y_sorted = _moe(x_sorted, moe_plan, w_gate_e, w_up_e, w_down_e, n_tiles)
    y_p = _combine(dest_row[:tp * TOP_K], y_sorted, x1_p, wt_p, CMB_TM)
    y_s = _combine(dest_row[tp * TOP_K:], y_sorted, x1_s, wt_s, min(CMB_TM, ts))

    v_out = lambda a, lead: a.reshape(lead + (2, N_HEADS, LANES)).swapaxes(-3, -2).reshape(lead + (N_HEADS, V_DIM))
    return (y_p.reshape(batch, seq, D_MODEL), y_s.reshape(nb, n_new, D_MODEL),
            kl_p.reshape(batch, seq, N_HEADS, 2, HEAD_DIM), v_out(vl_p, (batch, seq)),
            carry(hist_p, u_p),
            kl_s.reshape(nb, n_new, N_HEADS, 2, HEAD_DIM), v_out(vl_s, (nb, n_new)),
            carry(hist_s, u_s))


def kernel(x_prompt, x_sample, cache_k, cache_v, state_conv, page_table, norm1_g, w_in, conv_w, conv_b, conv_ln_g, conv_ln_b, w_conv_out, q_norm_g, k_norm_g, lambda_q1, lambda_k1, lambda_q2, lambda_k2, subln_g, w_attn_out, w_out, norm2_g, w_router_group, b_router_group, w_router_expert, b_router_expert, w_gate_e, w_up_e, w_down_e):
    depth = norm1_g.shape[0]
    x_p, x_s = x_prompt, x_sample
    outs = [[] for _ in range(6)]
    for l in range(depth):
        lambda_init = 0.8 - 0.6 * math.exp(-0.3 * l)
        x_p, x_s, *rest = _layer(
            x_p, x_s, cache_k[l], cache_v[l], state_conv[l], page_table, lambda_init,
            norm1_g[l], w_in[l], conv_w[l], conv_b[l], conv_ln_g[l], conv_ln_b[l], w_conv_out[l],
            q_norm_g[l], k_norm_g[l], lambda_q1[l], lambda_k1[l], lambda_q2[l], lambda_k2[l], subln_g[l],
            w_attn_out[l], w_out[l], norm2_g[l], w_router_group[l], b_router_group[l],
            w_router_expert[l], b_router_expert[l], w_gate_e[l], w_up_e[l], w_down_e[l])
        for acc, val in zip(outs, rest):
            acc.append(val)
    return (x_p, x_s) + tuple(jnp.stack(o) for o in outs)
```
